```python
import math, functools
import jax, jax.numpy as jnp
from jax import lax
import numpy as np

D_MODEL = 1024
BATCH = 8
SEQ = 2048
DEPTH = 4
DEC_BATCH = 128
DEC_SEQ = 8
PAST_LEN = 2048
PAGE_SIZE = 128

EPS = 1e-6
CHUNK = 128
GLA_CHUNK = 16
Q_BLOCK = 128
ROPE_BASE = 10000.0
H_A = 4
DK_A = 64
DV_A = 128
W_A = H_A * DV_A
H_B = 8
P_B = 64
N_B = 64
G_B = 2
CONV_W = 4
W_B = H_B * P_B
CONV_DIM = W_B + 2 * G_B * N_B
H_C = 8
DH_C = 64
W_C = H_C * DH_C
H_D = 4
DK_D = 64
DV_D = 128
W_D = H_D * DV_D
GLA_RANK = 16
GLA_TAU = 16.0
N_BRANCH = 4
BRANCH_W = W_A
N_MEM = 256
H_X = 4
DH_X = D_MODEL // H_X
D_FF = -(-8 * D_MODEL // (3 * 256)) * 256

IN_SIZES = (H_A * DK_A, H_A * DK_A, W_A, W_A,
            W_B, CONV_DIM, H_B,
            W_C, W_C, W_C, H_C,
            H_D * DK_D, H_D * DK_D, W_D, W_D, GLA_RANK,
            N_BRANCH * D_MODEL)
IN_COLS = sum(IN_SIZES)
IN_SPLITS = tuple(int(i) for i in np.cumsum(IN_SIZES)[:-1])

kernel_name = 'hybrid_ret_ssd_fox_gla_decoder_step'


def rmsnorm(x, g):
    xf = x.astype(jnp.float32)
    y = xf * lax.rsqrt(jnp.mean(xf * xf, axis=-1, keepdims=True) + EPS)
    return (y * g.astype(jnp.float32)).astype(x.dtype)


def head_norm(o, g, center):
    o = o.astype(jnp.float32)
    if center:
        o = o - jnp.mean(o, axis=-1, keepdims=True)
    o = o * lax.rsqrt(jnp.mean(o * o, axis=-1, keepdims=True) + EPS)
    return o * g.astype(jnp.float32)


def rotary(x, pos):
    half = x.shape[-1] // 2
    inv_freq = 1.0 / (ROPE_BASE ** (jnp.arange(half, dtype=jnp.float32) / half))
    ang = pos.astype(jnp.float32)[:, None] * inv_freq[None, :]
    cos = jnp.cos(ang)[None, :, None, :]
    sin = jnp.sin(ang)[None, :, None, :]
    xf = x.astype(jnp.float32)
    x1, x2 = xf[..., :half], xf[..., half:]
    return jnp.concatenate([x1 * cos - x2 * sin, x1 * sin + x2 * cos], axis=-1)


def causal_dwconv(u, prev, w, b):
    L = u.shape[1]
    padded = jnp.concatenate([prev.astype(u.dtype), u], axis=1)
    out = b + padded[:, 0:L] * w[0]
    for j in range(1, CONV_W):
        out = out + padded[:, j:j + L] * w[j]
    return out, padded[:, L:]


def chunked_recurrence(q, k, v, log_a, s0, chunk):
    f32 = jnp.float32
    bsz, L, H, K = q.shape
    V = v.shape[-1]
    n = L // chunk
    scalar = log_a.ndim == 3

    def split_chunks(t):
        return t.astype(f32).reshape((bsz, n, chunk) + t.shape[2:])

    q, k, v, la = (split_chunks(t) for t in (q, k, v, log_a))
    cum = jnp.cumsum(la, axis=2)
    causal = jnp.tril(jnp.ones((chunk, chunk), dtype=bool))
    to_end = jnp.exp(cum[:, :, -1:] - cum)
    from_start = jnp.exp(cum)
    if scalar:
        seg = cum[:, :, :, None, :] - cum[:, :, None, :, :]
        decay = jnp.exp(jnp.where(causal[None, None, :, :, None], seg, -jnp.inf))
        scores = jnp.einsum('bnthk,bnshk->bntsh', q, k) * decay
        q_in = q * from_start[..., None]
        k_out = k * to_end[..., None]
        chunk_decay = jnp.exp(cum[:, :, -1])[..., None, None]
    else:
        scores = jnp.einsum('bnthk,bnshk->bntsh', q * from_start, k * jnp.exp(-cum))
        scores = jnp.where(causal[None, None, :, :, None], scores, 0.0)
        q_in = q * from_start
        k_out = k * to_end
        chunk_decay = jnp.exp(cum[:, :, -1])[..., None]
    o_intra = jnp.einsum('bntsh,bnshv->bnthv', scores, v)
    ds = jnp.einsum('bnshk,bnshv->bnhkv', k_out, v)

    def step(s, inp):
        dec, d = inp
        return dec * s + d, s

    s_fin, s_prev = lax.scan(step, s0.astype(f32),
                             (jnp.moveaxis(chunk_decay, 1, 0), jnp.moveaxis(ds, 1, 0)))
    o_inter = jnp.einsum('bnthk,nbhkv->bnthv', q_in, s_prev)
    return (o_intra + o_inter).reshape(bsz, L, H, V), s_fin.astype(s0.dtype)


def retention_branch(q, k, v, gate, pos, s0, norm_g):
    bsz, L = q.shape[:2]
    q = rotary(q.reshape(bsz, L, H_A, DK_A), pos)
    k = rotary(k.reshape(bsz, L, H_A, DK_A), pos) * DK_A ** -0.5
    v = v.reshape(bsz, L, H_A, DV_A)
    log_gamma = jnp.log1p(-jnp.exp2(-5.0 - jnp.arange(H_A, dtype=jnp.float32)))
    la = jnp.broadcast_to(log_gamma, (bsz, L, H_A))
    o, s = chunked_recurrence(q, k, v, la, s0, math.gcd(L, CHUNK))
    o = head_norm(o, norm_g, center=True).reshape(bsz, L, W_A)
    y = jax.nn.silu(gate.astype(jnp.float32)) * o
    return y.astype(gate.dtype), s


def ssd_branch(z, xbc, dt, conv_prev, s0, conv_w, conv_b, dt_bias, a_log, d_skip, norm_g):
    bsz, L = z.shape[:2]
    u, conv_new = causal_dwconv(xbc, conv_prev, conv_w, conv_b)
    u = jax.nn.silu(u)
    xs, bm, cm = jnp.split(u, [W_B, W_B + G_B * N_B], axis=-1)
    xs = xs.reshape(bsz, L, H_B, P_B).astype(jnp.float32)
    rep = H_B // G_B
    bm = jnp.repeat(bm.reshape(bsz, L, G_B, N_B), rep, axis=2)
    cm = jnp.repeat(cm.reshape(bsz, L, G_B, N_B), rep, axis=2)
    dt = jax.nn.softplus(dt.astype(jnp.float32) + dt_bias.astype(jnp.float32))
    la = dt * (-jnp.exp(a_log.astype(jnp.float32)))
    o, s = chunked_recurrence(cm, bm, xs * dt[..., None], la, s0, math.gcd(L, CHUNK))
    o = o + d_skip.astype(jnp.float32)[:, None] * xs
    y = o.reshape(bsz, L, W_B) * jax.nn.silu(z.astype(jnp.float32))
    return rmsnorm(y, norm_g).astype(z.dtype), s, conv_new


def fox_prompt(q, k, v, log_f):
    bsz, L = q.shape[:2]
    scale = DH_C ** -0.5
    blk = math.gcd(L, Q_BLOCK)
    nb = L // blk
    c_t = jnp.cumsum(log_f.astype(jnp.float32), axis=1).transpose(0, 2, 1)
    qb = q.reshape(bsz, nb, blk, H_C, DH_C).swapaxes(0, 1)
    cb = c_t.reshape(bsz, H_C, nb, blk).transpose(2, 0, 1, 3)
    starts = jnp.arange(nb, dtype=jnp.int32) * blk
    kpos = jnp.arange(L, dtype=jnp.int32)

    def one_block(args):
        qi, ci, s0 = args
        logits = (jnp.einsum('bthd,bshd->bhts', qi, k).astype(jnp.float32) * scale
                  + ci[..., None] - c_t[:, :, None, :])
        qpos = s0 + jnp.arange(blk, dtype=jnp.int32)
        logits = jnp.where(kpos[None, :] <= qpos[:, None], logits, -jnp.inf)
        p = jax.nn.softmax(logits, axis=-1).astype(v.dtype)
        return jnp.einsum('bhts,bshd->bthd', p, v)

    out = lax.map(one_block, (qb, cb, starts))
    return out.swapaxes(0, 1).reshape(bsz, L, W_C)


def fox_sample(q, k, v, log_f, k_past, v_past, lf_past):
    bsz, T = q.shape[:2]
    scale = DH_C ** -0.5
    P = k_past.shape[1]
    c_past = jnp.cumsum(lf_past.astype(jnp.float32), axis=1)
    c_new = c_past[:, -1:] + jnp.cumsum(log_f.astype(jnp.float32), axis=1)
    cq = c_new.transpose(0, 2, 1)[..., None]
    s_past = (jnp.einsum('bthd,bshd->bhts', q, k_past).astype(jnp.float32) * scale
              + cq - c_past.transpose(0, 2, 1)[:, :, None, :])
    s_new = (jnp.einsum('bthd,bshd->bhts', q, k).astype(jnp.float32) * scale
             + cq - c_new.transpose(0, 2, 1)[:, :, None, :])
    s_new = jnp.where(jnp.tril(jnp.ones((T, T), dtype=bool)), s_new, -jnp.inf)
    p = jax.nn.softmax(jnp.concatenate([s_past, s_new], axis=-1), axis=-1).astype(v.dtype)
    out = (jnp.einsum('bhts,bshd->bthd', p[..., :P], v_past)
           + jnp.einsum('bhts,bshd->bthd', p[..., P:], v))
    return out.reshape(bsz, T, W_C)


def gla_branch(q, k, v, r, lr, s0, w_lr2, b_lr, norm_g):
    bsz, L = q.shape[:2]
    q = q.reshape(bsz, L, H_D, DK_D) * DK_D ** -0.5
    k = k.reshape(bsz, L, H_D, DK_D)
    v = v.reshape(bsz, L, H_D, DV_D)
    la = jax.nn.log_sigmoid((lr @ w_lr2 + b_lr).astype(jnp.float32)).reshape(bsz, L, H_D, DK_D) / GLA_TAU
    o, s = chunked_recurrence(q, k, v, la, s0, math.gcd(L, GLA_CHUNK))
    o = head_norm(o, norm_g, center=False).reshape(bsz, L, W_D)
    y = jax.nn.silu(r.astype(jnp.float32)) * o
    return y.astype(r.dtype), s


def token_mixers(xn, pos, s_ret, s_ssm, s_conv, s_gla, fox_fn, lp):
    bsz, L, _ = xn.shape
    (a_q, a_k, a_v, a_g, b_z, b_xbc, b_dt, c_q, c_k, c_v, c_f,
     d_q, d_k, d_v, d_r, d_lr, gate_logits) = jnp.split(xn @ lp['w_in'], IN_SPLITS, axis=-1)
    y_a, s_ret = retention_branch(a_q, a_k, a_v, a_g, pos, s_ret, lp['ret_norm_g'])
    y_b, s_ssm, s_conv = ssd_branch(b_z, b_xbc, b_dt, s_conv, s_ssm, lp['ssd_conv_w'], lp['ssd_conv_b'],
                                    lp['ssd_dt_bias'], lp['ssd_a_log'], lp['ssd_d'], lp['ssd_norm_g'])
    q_c = c_q.reshape(bsz, L, H_C, DH_C)
    k_c = c_k.reshape(bsz, L, H_C, DH_C)
    v_c = c_v.reshape(bsz, L, H_C, DH_C)
    log_f = jax.nn.log_sigmoid(c_f.astype(jnp.float32) + lp['fox_bf'].astype(jnp.float32))
    y_c = fox_fn(q_c, k_c, v_c, log_f).astype(xn.dtype)
    y_d, s_gla = gla_branch(d_q, d_k, d_v, d_r, d_lr, s_gla, lp['gla_w_lr2'], lp['gla_b_lr'], lp['gla_norm_g'])
    branches = jnp.stack([y_a, y_b, y_c, y_d], axis=2)
    proj = jnp.einsum('blnw,nwd->blnd', branches, lp['w_branch'])
    gates = jax.nn.sigmoid(gate_logits.reshape(bsz, L, N_BRANCH, D_MODEL) + lp['b_gate'])
    merged = jnp.sum(gates * proj, axis=2)
    return merged @ lp['w_out'], (s_ret, s_ssm, s_conv, s_gla, k_c, v_c, log_f)


def memory_kv(mem, g, w_kv):
    kv = rmsnorm(mem, g) @ w_kv
    k, v = jnp.split(kv, 2, axis=-1)
    shp = mem.shape[:2] + (H_X, DH_X)
    return k.reshape(shp), v.reshape(shp)


def cross_attn(hn, mem_k, mem_v, w_q, w_o):
    bsz, L, _ = hn.shape
    q = (hn @ w_q).reshape(bsz, L, H_X, DH_X)
    logits = jnp.einsum('blhd,bmhd->bhlm', q, mem_k).astype(jnp.float32) * DH_X ** -0.5
    p = jax.nn.softmax(logits, axis=-1).astype(mem_v.dtype)
    o = jnp.einsum('bhlm,bmhd->blhd', p, mem_v).reshape(bsz, L, D_MODEL)
    return o @ w_o


def swiglu(hn, w_up, w_down):
    a, b = jnp.split(hn @ w_up, 2, axis=-1)
    return (jax.nn.silu(a) * b) @ w_down


def block(h, pos, s_ret, s_ssm, s_conv, s_gla, fox_fn, mem_k, mem_v, lp):
    mix, new = token_mixers(rmsnorm(h, lp['norm_mix_g']), pos, s_ret, s_ssm, s_conv, s_gla, fox_fn, lp)
    h = h + mix
    h = h + cross_attn(rmsnorm(h, lp['norm_x_g']), mem_k, mem_v, lp['w_xq'], lp['w_xo'])
    h = h + swiglu(rmsnorm(h, lp['norm_ffn_g']), lp['w_ffn_up'], lp['w_ffn_down'])
    return h, new


def setup_inputs(seed: int = 0) -> dict:
    key = jax.random.key(seed)
    ks = iter(jax.random.split(key, 64))
    f32 = jnp.float32

    def nrm(shape, scale):
        return scale * jax.random.normal(next(ks), shape, f32)

    def gain(shape):
        return 1.0 + 0.02 * jax.random.normal(next(ks), shape, f32)

    n_pages = PAST_LEN // PAGE_SIZE
    n_used = DEC_BATCH * n_pages
    n_pool = n_used + max(1, n_used // 4)
    page_table = jax.random.permutation(next(ks), n_pool)[:n_used].reshape(DEC_BATCH, n_pages).astype(jnp.int32)
    dt0 = jnp.exp(jax.random.uniform(next(ks), (DEPTH, H_B), f32, math.log(1e-3), math.log(1e-1)))
    dt_bias = dt0 + jnp.log(-jnp.expm1(-dt0))
    a_log = jnp.log(jax.random.uniform(next(ks), (DEPTH, H_B), f32, 1.0, 16.0))
    return {
        'x_prompt': nrm((BATCH, SEQ, D_MODEL), 1.0),
        'x_sample': nrm((DEC_BATCH, DEC_SEQ, D_MODEL), 1.0),
        'cache_fox_k': nrm((DEPTH, n_pool, PAGE_SIZE, H_C, DH_C), 1.0),
        'cache_fox_v': nrm((DEPTH, n_pool, PAGE_SIZE, H_C, DH_C), 1.0),
        'cache_fox_logf': jax.nn.log_sigmoid(nrm((DEPTH, n_pool, PAGE_SIZE, H_C), 1.0) + 2.0),
        'cache_mem_k': nrm((DEPTH, DEC_BATCH, N_MEM, H_X, DH_X), 1.0),
        'cache_mem_v': nrm((DEPTH, DEC_BATCH, N_MEM, H_X, DH_X), 1.0),
        'state_ret': nrm((DEPTH, DEC_BATCH, H_A, DK_A, DV_A), 1.0),
        'state_ssm': nrm((DEPTH, DEC_BATCH, H_B, N_B, P_B), 1.0),
        'state_conv': nrm((DEPTH, DEC_BATCH, CONV_W - 1, CONV_DIM), 1.0),
        'state_gla': nrm((DEPTH, DEC_BATCH, H_D, DK_D, DV_D), 1.0),
        'page_table': page_table,
        'mem_prompt': nrm((BATCH, N_MEM, D_MODEL), 1.0),
        'w_in': nrm((DEPTH, D_MODEL, IN_COLS), D_MODEL ** -0.5),
        'b_gate': nrm((DEPTH, N_BRANCH, D_MODEL), 0.01),
        'ret_norm_g': gain((DEPTH, H_A, DV_A)),
        'ssd_conv_w': nrm((DEPTH, CONV_W, CONV_DIM), CONV_W ** -0.5),
        'ssd_conv_b': nrm((DEPTH, CONV_DIM), 0.01),
        'ssd_dt_bias': dt_bias,
        'ssd_a_log': a_log,
        'ssd_d': gain((DEPTH, H_B)),
        'ssd_norm_g': gain((DEPTH, W_B)),
        'fox_bf': 2.0 + nrm((DEPTH, H_C), 0.5),
        'gla_w_lr2': nrm((DEPTH, GLA_RANK, H_D * DK_D), GLA_RANK ** -0.5),
        'gla_b_lr': nrm((DEPTH, H_D * DK_D), 0.01),
        'gla_norm_g': gain((DEPTH, H_D, DV_D)),
        'w_branch': nrm((DEPTH, N_BRANCH, BRANCH_W, D_MODEL), BRANCH_W ** -0.5),
        'w_out': nrm((DEPTH, D_MODEL, D_MODEL), D_MODEL ** -0.5),
        'norm_mix_g': gain((DEPTH, D_MODEL)),
        'norm_x_g': gain((DEPTH, D_MODEL)),
        'norm_mem_g': gain((DEPTH, D_MODEL)),
        'w_xq': nrm((DEPTH, D_MODEL, D_MODEL), D_MODEL ** -0.5),
        'w_mem_kv': nrm((DEPTH, D_MODEL, 2 * D_MODEL), D_MODEL ** -0.5),
        'w_xo': nrm((DEPTH, D_MODEL, D_MODEL), D_MODEL ** -0.5),
        'norm_ffn_g': gain((DEPTH, D_MODEL)),
        'w_ffn_up': nrm((DEPTH, D_MODEL, 2 * D_FF), D_MODEL ** -0.5),
        'w_ffn_down': nrm((DEPTH, D_FF, D_MODEL), D_FF ** -0.5),
        'norm_final_g': gain((D_MODEL,)),
    }


def reference(x_prompt, x_sample, cache_fox_k, cache_fox_v, cache_fox_logf, cache_mem_k, cache_mem_v,
              state_ret, state_ssm, state_conv, state_gla, page_table, mem_prompt,
              w_in, b_gate, ret_norm_g, ssd_conv_w, ssd_conv_b, ssd_dt_bias, ssd_a_log, ssd_d, ssd_norm_g,
              fox_bf, gla_w_lr2, gla_b_lr, gla_norm_g, w_branch, w_out, norm_mix_g, norm_x_g, norm_mem_g,
              w_xq, w_mem_kv, w_xo, norm_ffn_g, w_ffn_up, w_ffn_down, norm_final_g):
    bp, lp_len = x_prompt.shape[0], x_prompt.shape[1]
    bs, ls_len = x_sample.shape[0], x_sample.shape[1]
    pos_p = jnp.arange(lp_len, dtype=jnp.int32)
    pos_s = PAST_LEN + jnp.arange(ls_len, dtype=jnp.int32)
    hp, hs = x_prompt, x_sample
    p_ret, p_ssm, p_conv, p_gla, p_fk, p_fv, p_flf, p_mk, p_mv = [], [], [], [], [], [], [], [], []
    s_ret, s_ssm, s_conv, s_gla, s_fk, s_fv, s_flf = [], [], [], [], [], [], []
    for l in range(DEPTH):
        lp = {'w_in': w_in[l], 'b_gate': b_gate[l], 'ret_norm_g': ret_norm_g[l],
              'ssd_conv_w': ssd_conv_w[l], 'ssd_conv_b': ssd_conv_b[l], 'ssd_dt_bias': ssd_dt_bias[l],
              'ssd_a_log': ssd_a_log[l], 'ssd_d': ssd_d[l], 'ssd_norm_g': ssd_norm_g[l],
              'fox_bf': fox_bf[l], 'gla_w_lr2': gla_w_lr2[l], 'gla_b_lr': gla_b_lr[l],
              'gla_norm_g': gla_norm_g[l], 'w_branch': w_branch[l], 'w_out': w_out[l],
              'norm_mix_g': norm_mix_g[l], 'norm_x_g': norm_x_g[l], 'w_xq': w_xq[l], 'w_xo': w_xo[l],
              'norm_ffn_g': norm_ffn_g[l], 'w_ffn_up': w_ffn_up[l], 'w_ffn_down': w_ffn_down[l]}
        z_ret = jnp.zeros((bp, H_A, DK_A, DV_A), jnp.float32)
        z_ssm = jnp.zeros((bp, H_B, N_B, P_B), jnp.float32)
        z_conv = jnp.zeros((bp, CONV_W - 1, CONV_DIM), x_prompt.dtype)
        z_gla = jnp.zeros((bp, H_D, DK_D, DV_D), jnp.float32)
        mk, mv = memory_kv(mem_prompt, norm_mem_g[l], w_mem_kv[l])
        hp, new_p = block(hp, pos_p, z_ret, z_ssm, z_conv, z_gla, fox_prompt, mk, mv, lp)
        r_, m_, c_, g_, fk_, fv_, flf_ = new_p
        p_ret.append(r_); p_ssm.append(m_); p_conv.append(c_); p_gla.append(g_)
        p_fk.append(fk_); p_fv.append(fv_); p_flf.append(flf_); p_mk.append(mk); p_mv.append(mv)
        k_past = cache_fox_k[l][page_table].reshape(bs, -1, H_C, DH_C)
        v_past = cache_fox_v[l][page_table].reshape(bs, -1, H_C, DH_C)
        lf_past = cache_fox_logf[l][page_table].reshape(bs, -1, H_C)
        fox_fn = functools.partial(fox_sample, k_past=k_past, v_past=v_past, lf_past=lf_past)
        hs, new_s = block(hs, pos_s, state_ret[l], state_ssm[l], state_conv[l], state_gla[l],
                          fox_fn, cache_mem_k[l], cache_mem_v[l], lp)
        r_, m_, c_, g_, fk_, fv_, flf_ = new_s
        s_ret.append(r_); s_ssm.append(m_); s_conv.append(c_); s_gla.append(g_)
        s_fk.append(fk_); s_fv.append(fv_); s_flf.append(flf_)
    y_prompt = rmsnorm(hp, norm_final_g)
    y_sample = rmsnorm(hs, norm_final_g)
    new_ret_p = jnp.stack(p_ret, axis=0)
    new_ssm_p = jnp.stack(p_ssm, axis=0)
    new_conv_p = jnp.stack(p_conv, axis=0)
    new_gla_p = jnp.stack(p_gla, axis=0)
    new_fox_k_p = jnp.stack(p_fk, axis=0)
    new_fox_v_p = jnp.stack(p_fv, axis=0)
    new_fox_logf_p = jnp.stack(p_flf, axis=0)
    new_mem_k_p = jnp.stack(p_mk, axis=0)
    new_mem_v_p = jnp.stack(p_mv, axis=0)
    new_ret_s = jnp.stack(s_ret, axis=0)
    new_ssm_s = jnp.stack(s_ssm, axis=0)
    new_conv_s = jnp.stack(s_conv, axis=0)
    new_gla_s = jnp.stack(s_gla, axis=0)
    new_fox_k_s = jnp.stack(s_fk, axis=0)
    new_fox_v_s = jnp.stack(s_fv, axis=0)
    new_fox_logf_s = jnp.stack(s_flf, axis=0)
    return (y_prompt, y_sample,
            new_ret_p, new_ssm_p, new_conv_p, new_gla_p, new_fox_k_p, new_fox_v_p, new_fox_logf_p,
            new_mem_k_p, new_mem_v_p,
            new_ret_s, new_ssm_s, new_conv_s, new_gla_s, new_fox_k_s, new_fox_v_s, new_fox_logf_s)
```

```python
import functools
import math

import jax
import jax.numpy as jnp
from jax import lax
from jax.experimental import pallas as pl
from jax.experimental.pallas import tpu as pltpu

f32 = jnp.float32
bf16 = jnp.bfloat16

EPS = 1e-6
CHUNK = 128
GLA_CHUNK = 16
GLA_SUB = 16
ROPE_BASE = 10000.0
PAGE_SIZE = 128
H_A, DK_A, DV_A = 4, 64, 128
H_B, P_B, N_B, G_B, CONV_W = 8, 64, 64, 2, 4
H_C, DH_C = 8, 64
H_D, DK_D, DV_D = 4, 64, 128
GLA_RANK = 16
GLA_TAU = 16.0
N_BRANCH = 4
H_X = 4
BW = 512
CONV_DIM = BW + 2 * G_B * N_B

C_AV, C_AG, C_BZ, C_BXS, C_CQ, C_CK, C_CV, C_DV, C_DR = range(9)
C_AQ, C_AK, C_BBC, C_DQ, C_DK = range(18, 23)
C_SMALL = 46
NP_COLS = 6144
LANE = 128
VMEM_LIMIT = 56 * 1024 * 1024


def _cp(*sem):
    return pltpu.CompilerParams(dimension_semantics=sem, vmem_limit_bytes=VMEM_LIMIT)


def _pick(n, cands):
    for c in cands:
        if n % c == 0:
            return c
    raise ValueError(f"no tile for {n}")


def _dot(a, b):
    return jnp.dot(a.astype(bf16), b.astype(bf16), preferred_element_type=f32)


def _dot_nt(a, b):
    return lax.dot_general(a.astype(bf16), b.astype(bf16), (((1,), (1,)), ((), ())),
                           preferred_element_type=f32)


def _dot_tn(a, b):
    return lax.dot_general(a.astype(bf16), b.astype(bf16), (((0,), (0,)), ((), ())),
                           preferred_element_type=f32)


def _silu(x):
    return x * jax.nn.sigmoid(x)


def _log_sigmoid(x):
    return jnp.minimum(x, 0.0) - jnp.log1p(jnp.exp(-jnp.abs(x)))


def _softplus(x):
    return jnp.maximum(x, 0.0) + jnp.log1p(jnp.exp(-jnp.abs(x)))


def _cumsum_rows(x):
    n = x.shape[0]
    row = lax.broadcasted_iota(jnp.int32, x.shape, 0)
    k = 1
    while k < n:
        x = x + jnp.where(row >= k, pltpu.roll(x, k, axis=0), 0.0)
        k *= 2
    return x


def _rms(x, g):
    return x * lax.rsqrt(jnp.mean(x * x, axis=-1, keepdims=True) + EPS) * g


def _norm_matmul_kernel(x_ref, g_ref, w_ref, o_ref, xn_ref):
    @pl.when(pl.program_id(1) == 0)
    def _():
        xn_ref[...] = _rms(x_ref[...], g_ref[...]).astype(bf16)

    o_ref[...] = jnp.dot(xn_ref[...], w_ref[...], preferred_element_type=f32).astype(o_ref.dtype)


def _norm_matmul(x, g, w, l, *, tn, out_dtype=f32, split=False):
    M, K = x.shape
    N = w.shape[2]
    tm = _pick(M, (1024, 512, 256, 128, 64, 32, 16, 8))
    if split:
        out_shape = jax.ShapeDtypeStruct((N // tn, M, tn), out_dtype)
        out_spec = pl.BlockSpec((None, tm, tn), lambda i, j: (j, i, 0))
    else:
        out_shape = jax.ShapeDtypeStruct((M, N), out_dtype)
        out_spec = pl.BlockSpec((tm, tn), lambda i, j: (i, j))
    return pl.pallas_call(
        _norm_matmul_kernel,
        grid=(M // tm, N // tn),
        in_specs=[pl.BlockSpec((tm, K), lambda i, j: (i, 0)),
                  pl.BlockSpec((None, 1, K), lambda i, j: (l, 0, 0)),
                  pl.BlockSpec((None, K, tn), lambda i, j: (l, 0, j))],
        out_specs=out_spec,
        out_shape=out_shape,
        scratch_shapes=[pltpu.VMEM((tm, K), bf16)],
        compiler_params=_cp("parallel", "arbitrary"),
        name="norm_matmul",
    )(x, g, w)


def _two_src(tm, width, n_p, nargs):
    if nargs == 1:
        return (pl.BlockSpec((tm, width), lambda m: (jnp.minimum(m, n_p - 1), 0)),
                pl.BlockSpec((tm, width), lambda m: (jnp.maximum(m - n_p, 0), 0)))
    return (pl.BlockSpec((tm, width), lambda m, i: (jnp.minimum(m, n_p - 1), 0)),
            pl.BlockSpec((tm, width), lambda m, i: (jnp.maximum(m - n_p, 0), 0)))


def _merge_kernel(h_ref, g_ref, *refs, n_p):
    yp = refs[0:4]
    ys = refs[4:8]
    wg_ref, bg_ref, wb_ref, wo_ref, o_ref, xn_ref, acc_ref = refs[8:]
    m = pl.program_id(0)
    i = pl.program_id(1)

    @pl.when(i == 0)
    def _():
        xn_ref[...] = _rms(h_ref[...], g_ref[...]).astype(bf16)
        acc_ref[...] = jnp.zeros_like(acc_ref)

    gate = jax.nn.sigmoid(jnp.dot(xn_ref[...], wg_ref[...], preferred_element_type=f32) + bg_ref[...])
    for k in range(N_BRANCH):
        @pl.when(i == k)
        def _(k=k):
            y = jnp.where(m < n_p, yp[k][...].astype(bf16), ys[k][...].astype(bf16))
            acc_ref[...] += gate * jnp.dot(y, wb_ref[...], preferred_element_type=f32)

    @pl.when(i == N_BRANCH - 1)
    def _():
        o_ref[...] = h_ref[...] + jnp.dot(acc_ref[...].astype(bf16), wo_ref[...],
                                          preferred_element_type=f32)


def _merge(h, g, yps, yss, w_gate, b_gate, w_branch, w_out, l):
    T, D = h.shape
    Tp, Ts = yps[0].shape[0], yss[0].shape[0]
    tm = _pick(math.gcd(Tp, Ts), (512, 256, 128, 64, 32, 16, 8))
    n_p = Tp // tm
    sp, ss = _two_src(tm, BW, n_p, 2)
    return pl.pallas_call(
        functools.partial(_merge_kernel, n_p=n_p),
        grid=(T // tm, N_BRANCH),
        in_specs=[pl.BlockSpec((tm, D), lambda m, i: (m, 0)),
                  pl.BlockSpec((None, 1, D), lambda m, i: (l, 0, 0))]
                 + [sp] * 4 + [ss] * 4
                 + [pl.BlockSpec((None, D, D), lambda m, i: (l, 0, i)),
                    pl.BlockSpec((None, None, 1, D), lambda m, i: (l, i, 0, 0)),
                    pl.BlockSpec((None, None, BW, D), lambda m, i: (l, i, 0, 0)),
                    pl.BlockSpec((None, D, D), lambda m, i: (l, 0, 0))],
        out_specs=pl.BlockSpec((tm, D), lambda m, i: (m, 0)),
        out_shape=jax.ShapeDtypeStruct((T, D), f32),
        scratch_shapes=[pltpu.VMEM((tm, D), bf16), pltpu.VMEM((tm, D), f32)],
        compiler_params=_cp("parallel", "arbitrary"),
        name="branch_merge",
    )(h, g, *yps, *yss, w_gate, b_gate, w_branch, w_out)


def _xo_kernel(h_ref, op_ref, os_ref, w_ref, o_ref, *, n_p):
    m = pl.program_id(0)
    o = jnp.where(m < n_p, op_ref[...].astype(bf16), os_ref[...].astype(bf16))
    o_ref[...] = h_ref[...] + jnp.dot(o, w_ref[...], preferred_element_type=f32)


def _xo(h, o_p, o_s, w, l):
    T, D = h.shape
    Tp, Ts = o_p.shape[0], o_s.shape[0]
    tm = _pick(math.gcd(Tp, Ts), (1024, 512, 256, 128, 64, 32, 16, 8))
    n_p = Tp // tm
    sp, ss = _two_src(tm, D, n_p, 1)
    return pl.pallas_call(
        functools.partial(_xo_kernel, n_p=n_p),
        grid=(T // tm,),
        in_specs=[pl.BlockSpec((tm, D), lambda m: (m, 0)), sp, ss,
                  pl.BlockSpec((None, D, D), lambda m: (l, 0, 0))],
        out_specs=pl.BlockSpec((tm, D), lambda m: (m, 0)),
        out_shape=jax.ShapeDtypeStruct((T, D), f32),
        compiler_params=_cp("parallel"),
        name="xattn_out",
    )(h, o_p, o_s, w)


def _ffn_kernel(h_ref, g_ref, wa_ref, wb_ref, wd_ref, o_ref, xn_ref, acc_ref, *, nj):
    j = pl.program_id(1)

    @pl.when(j == 0)
    def _():
        xn_ref[...] = _rms(h_ref[...], g_ref[...]).astype(bf16)
        acc_ref[...] = jnp.zeros_like(acc_ref)

    xn = xn_ref[...]
    a = jnp.dot(xn, wa_ref[...], preferred_element_type=f32)
    b = jnp.dot(xn, wb_ref[...], preferred_element_type=f32)
    acc_ref[...] += jnp.dot((_silu(a) * b).astype(bf16), wd_ref[...], preferred_element_type=f32)

    @pl.when(j == nj - 1)
    def _():
        o_ref[...] = h_ref[...] + acc_ref[...]


def _ffn(h, g, w_up, w_down, l):
    T, D = h.shape
    d_ff = w_down.shape[1]
    tm = _pick(T, (1024, 512, 256, 128, 64, 32, 16, 8))
    tf = _pick(d_ff, (256, 128))
    nj = d_ff // tf
    return pl.pallas_call(
        functools.partial(_ffn_kernel, nj=nj),
        grid=(T // tm, nj),
        in_specs=[pl.BlockSpec((tm, D), lambda m, j: (m, 0)),
                  pl.BlockSpec((None, 1, D), lambda m, j: (l, 0, 0)),
                  pl.BlockSpec((None, D, tf), lambda m, j: (l, 0, j)),
                  pl.BlockSpec((None, D, tf), lambda m, j: (l, 0, j + nj)),
                  pl.BlockSpec((None, tf, D), lambda m, j: (l, j, 0))],
        out_specs=pl.BlockSpec((tm, D), lambda m, j: (m, 0)),
        out_shape=jax.ShapeDtypeStruct((T, D), f32),
        scratch_shapes=[pltpu.VMEM((tm, D), bf16), pltpu.VMEM((tm, D), f32)],
        compiler_params=_cp("parallel", "arbitrary"),
        name="swiglu_ffn",
    )(h, g, w_up, w_up, w_down)


def _final_norm_kernel(x_ref, g_ref, o_ref):
    o_ref[...] = _rms(x_ref[...], g_ref[...])


def _final_norm(h, g):
    T, D = h.shape
    tm = _pick(T, (1024, 512, 256, 128, 64, 32, 16, 8))
    return pl.pallas_call(
        _final_norm_kernel,
        grid=(T // tm,),
        in_specs=[pl.BlockSpec((tm, D), lambda m: (m, 0)), pl.BlockSpec((1, D), lambda m: (0, 0))],
        out_specs=pl.BlockSpec((tm, D), lambda m: (m, 0)),
        out_shape=jax.ShapeDtypeStruct((T, D), f32),
        compiler_params=_cp("parallel"),
        name="final_norm",
    )(h, g)


def _ret_kernel(q_ref, k_ref, v_ref, g_ref, cos_ref, sin_ref, s0_ref, ng_ref, y_ref, sf_ref, S_ref,
                *, C, nch):
    n = pl.program_id(1)

    @pl.when(n == 0)
    def _():
        S_ref[...] = s0_ref[...]

    cos = cos_ref[...]
    sin = sin_ref[...]
    lane = lax.broadcasted_iota(jnp.int32, (C, H_A * DK_A), 1)
    first_half = (lane % DK_A) < (DK_A // 2)

    def rot(x):
        fwd = pltpu.roll(x, DK_A // 2, axis=1)
        bwd = pltpu.roll(x, H_A * DK_A - DK_A // 2, axis=1)
        return x * cos + jnp.where(first_half, bwd, fwd) * sin

    q = rot(q_ref[...])
    k = rot(k_ref[...]) * (DK_A ** -0.5)
    v = v_ref[...]
    gate = g_ref[...]
    ng = ng_ref[...]
    t_col = lax.broadcasted_iota(jnp.int32, (C, 1), 0).astype(f32)
    diff = (lax.broadcasted_iota(jnp.int32, (C, C), 0)
            - lax.broadcasted_iota(jnp.int32, (C, C), 1))
    for h in range(H_A):
        lg = math.log1p(-2.0 ** (-5.0 - h))
        qh = q[:, h * DK_A:(h + 1) * DK_A]
        kh = k[:, h * DK_A:(h + 1) * DK_A]
        vh = v[:, h * DV_A:(h + 1) * DV_A]
        decay = jnp.where(diff >= 0, jnp.exp(diff.astype(f32) * lg), 0.0)
        scores = _dot_nt(qh, kh) * decay
        s_prev = S_ref[h]
        o = _dot(scores, vh) + _dot(qh * jnp.exp((t_col + 1.0) * lg), s_prev)
        k_out = kh * jnp.exp((C - 1.0 - t_col) * lg)
        S_ref[h] = math.exp(C * lg) * s_prev + _dot_tn(k_out, vh)
        o = o - jnp.mean(o, axis=-1, keepdims=True)
        o = o * lax.rsqrt(jnp.mean(o * o, axis=-1, keepdims=True) + EPS)
        o = o * ng[:, h * DV_A:(h + 1) * DV_A]
        y_ref[:, h * DV_A:(h + 1) * DV_A] = (_silu(gate[:, h * DV_A:(h + 1) * DV_A]) * o).astype(y_ref.dtype)

    @pl.when(n == nch - 1)
    def _():
        sf_ref[...] = S_ref[...]


def _retention(proj, cos, sin, s0, ls, ng, l, *, B, L, row_off, out_dtype):
    C = math.gcd(L, CHUNK)
    nch = L // C
    r0 = row_off // C
    row = lambda b, n: r0 + b * nch + n
    return pl.pallas_call(
        functools.partial(_ret_kernel, C=C, nch=nch),
        grid=(B, nch),
        in_specs=[pl.BlockSpec((C, 256), lambda b, n: (row(b, n), C_AQ)),
                  pl.BlockSpec((C, 256), lambda b, n: (row(b, n), C_AK)),
                  pl.BlockSpec((C, 512), lambda b, n: (row(b, n), C_AV)),
                  pl.BlockSpec((C, 512), lambda b, n: (row(b, n), C_AG)),
                  pl.BlockSpec((C, 256), lambda b, n: (n, 0)),
                  pl.BlockSpec((C, 256), lambda b, n: (n, 0)),
                  pl.BlockSpec((None, None, H_A, DK_A, DV_A), lambda b, n: (ls, b, 0, 0, 0)),
                  pl.BlockSpec((None, 1, BW), lambda b, n: (l, 0, 0))],
        out_specs=[pl.BlockSpec((C, BW), lambda b, n: (b * nch + n, 0)),
                   pl.BlockSpec((None, H_A, DK_A, DV_A), lambda b, n: (b, 0, 0, 0))],
        out_shape=[jax.ShapeDtypeStruct((B * L, BW), out_dtype),
                   jax.ShapeDtypeStruct((B, H_A, DK_A, DV_A), f32)],
        scratch_shapes=[pltpu.VMEM((H_A, DK_A, DV_A), f32)],
        compiler_params=_cp("parallel", "arbitrary"),
        name="retention",
    )(proj, proj, proj, proj, cos, sin, s0, ng)


def _ssd_kernel(z_ref, xs_ref, bc_ref, sm_ref, c0_ref, s0_ref, cw_ref, cb_ref, dtb_ref, al_ref,
                dsk_ref, ng_ref, y_ref, cn_ref, sf_ref, xpad_ref, S_ref, o_ref, *, C, nch):
    n = pl.program_id(1)
    P0 = 8 - (CONV_W - 1)

    @pl.when(n == 0)
    def _():
        xpad_ref[P0:8, :] = c0_ref[...]
        S_ref[...] = s0_ref[...]

    xpad_ref[8:8 + C, 0:BW] = xs_ref[...]
    xpad_ref[8:8 + C, BW:CONV_DIM] = bc_ref[...]
    cw = cw_ref[...]
    conv = cb_ref[...] + xpad_ref[P0:P0 + C, :] * cw[0:1, :]
    for j in range(1, CONV_W):
        conv = conv + xpad_ref[P0 + j:P0 + j + C, :] * cw[j:j + 1, :]
    tail = xpad_ref[C + P0:C + 8, :]
    cn_ref[...] = tail
    xpad_ref[P0:8, :] = tail

    u = _silu(conv)
    xs = u[:, 0:BW]
    dt = _softplus(sm_ref[...] + dtb_ref[...])
    la = dt * (-jnp.exp(al_ref[...]))
    cum = _cumsum_rows(la)
    cum_t = cum.T
    cum_last = cum[C - 1:C, :]
    causal = (lax.broadcasted_iota(jnp.int32, (C, C), 0)
              >= lax.broadcasted_iota(jnp.int32, (C, C), 1))
    dsk = dsk_ref[...]
    rep = H_B // G_B
    for g in range(G_B):
        bm = u[:, BW + g * N_B:BW + (g + 1) * N_B]
        cm = u[:, BW + G_B * N_B + g * N_B:BW + G_B * N_B + (g + 1) * N_B]
        scores_g = _dot_nt(cm, bm)
        for h in range(g * rep, (g + 1) * rep):
            cum_h = cum[:, h:h + 1]
            seg = cum_h - cum_t[h:h + 1, :]
            decay = jnp.exp(jnp.where(causal, seg, -jnp.inf))
            xh = xs[:, h * P_B:(h + 1) * P_B]
            vh = xh * dt[:, h:h + 1]
            s_prev = S_ref[h]
            o = _dot(scores_g * decay, vh) + _dot(cm * jnp.exp(cum_h), s_prev)
            k_out = bm * jnp.exp(cum_last[:, h:h + 1] - cum_h)
            S_ref[h] = jnp.exp(cum_last[:, h:h + 1]) * s_prev + _dot_tn(k_out, vh)
            o_ref[:, h * P_B:(h + 1) * P_B] = o + dsk[:, h * P_B:(h + 1) * P_B] * xh
    y = o_ref[...] * _silu(z_ref[...])
    y_ref[...] = _rms(y, ng_ref[...]).astype(y_ref.dtype)

    @pl.when(n == nch - 1)
    def _():
        sf_ref[...] = S_ref[...]


def _ssd(proj, c0, s0, ls, cw, cb, dtb, al, dsk, ng, l, *, B, L, row_off, out_dtype):
    C = math.gcd(L, CHUNK)
    nch = L // C
    r0 = row_off // C
    row = lambda b, n: r0 + b * nch + n
    par = lambda shape: pl.BlockSpec((None,) + shape, lambda b, n: (l,) + (0,) * len(shape))
    return pl.pallas_call(
        functools.partial(_ssd_kernel, C=C, nch=nch),
        grid=(B, nch),
        in_specs=[pl.BlockSpec((C, 512), lambda b, n: (row(b, n), C_BZ)),
                  pl.BlockSpec((C, 512), lambda b, n: (row(b, n), C_BXS)),
                  pl.BlockSpec((C, 256), lambda b, n: (row(b, n), C_BBC)),
                  pl.BlockSpec((C, 128), lambda b, n: (row(b, n), C_SMALL)),
                  pl.BlockSpec((None, None, CONV_W - 1, CONV_DIM), lambda b, n: (ls, b, 0, 0)),
                  pl.BlockSpec((None, None, H_B, N_B, P_B), lambda b, n: (ls, b, 0, 0, 0)),
                  par((CONV_W, CONV_DIM)), par((1, CONV_DIM)), par((1, LANE)), par((1, LANE)),
                  par((1, BW)), par((1, BW))],
        out_specs=[pl.BlockSpec((C, BW), lambda b, n: (b * nch + n, 0)),
                   pl.BlockSpec((None, CONV_W - 1, CONV_DIM), lambda b, n: (b, 0, 0)),
                   pl.BlockSpec((None, H_B, N_B, P_B), lambda b, n: (b, 0, 0, 0))],
        out_shape=[jax.ShapeDtypeStruct((B * L, BW), out_dtype),
                   jax.ShapeDtypeStruct((B, CONV_W - 1, CONV_DIM), f32),
                   jax.ShapeDtypeStruct((B, H_B, N_B, P_B), f32)],
        scratch_shapes=[pltpu.VMEM((C + 8, CONV_DIM), f32), pltpu.VMEM((H_B, N_B, P_B), f32),
                        pltpu.VMEM((C, BW), f32)],
        compiler_params=_cp("parallel", "arbitrary"),
        name="ssd",
    )(proj, proj, proj, proj, c0, s0, cw, cb, dtb, al, dsk, ng)


def _gla_kernel(q_ref, k_ref, v_ref, r_ref, sm_ref, s0_ref, w2_ref, b2_ref, ng_ref, y_ref, sf_ref,
                S_ref, *, C, c, nch):
    n = pl.program_id(1)
    nsub = C // c
    HK = H_D * DK_D

    @pl.when(n == 0)
    def _():
        S_ref[...] = s0_ref[...]

    x = jnp.dot(sm_ref[...].astype(bf16), w2_ref[...], preferred_element_type=f32) + b2_ref[...]
    la = _log_sigmoid(x) / GLA_TAU
    cum = _cumsum_rows(la)
    cum_last = cum[C - 1:C, :]
    qs = q_ref[...] * (DK_D ** -0.5)
    k = k_ref[...]
    v = v_ref[...]
    r = r_ref[...]
    ng = ng_ref[...]
    q_in = qs * jnp.exp(cum)
    k_out = k * jnp.exp(cum_last - cum)
    dec_col = jnp.broadcast_to(jnp.exp(cum_last), (8, HK)).T[:, 0:1]

    row = lax.broadcasted_iota(jnp.int32, (C, LANE), 0)
    lane = lax.broadcasted_iota(jnp.int32, (C, LANE), 1)
    causal = (lax.broadcasted_iota(jnp.int32, (C, C), 0)
              >= lax.broadcasted_iota(jnp.int32, (C, C), 1))
    for p in range(HK // LANE):
        sl = slice(p * LANE, (p + 1) * LANE)
        cum_p = cum[:, sl]
        k_p = k[:, sl]
        ref_row = jnp.zeros((C, LANE), f32)
        k_pieces = []
        for i in range(nsub):
            ref_i = cum_p[i * c - 1:i * c, :] if i > 0 else jnp.zeros((1, LANE), f32)
            in_blk = (row >= i * c) & (row < (i + 1) * c)
            ref_row = jnp.where(in_blk, ref_i, ref_row)
            expo = jnp.where(row < (i + 1) * c, ref_i - cum_p, -jnp.inf)
            k_pieces.append((k_p * jnp.exp(expo)).astype(bf16))
        k_big = jnp.concatenate(k_pieces, axis=1) if nsub > 1 else k_pieces[0]
        q_sc = qs[:, sl] * jnp.exp(cum_p - ref_row)
        for hh in range(LANE // DK_D):
            h = p * (LANE // DK_D) + hh
            in_head = (lane >= hh * DK_D) & (lane < (hh + 1) * DK_D)
            q_pieces = []
            for i in range(nsub):
                in_blk = (row >= i * c) & (row < (i + 1) * c)
                q_pieces.append(jnp.where(in_blk & in_head, q_sc, 0.0).astype(bf16))
            q_big = jnp.concatenate(q_pieces, axis=1) if nsub > 1 else q_pieces[0]
            scores = jnp.where(causal, _dot_nt(q_big, k_big), 0.0)
            vh = v[:, h * DV_D:(h + 1) * DV_D]
            s_prev = S_ref[h]
            hs = slice(h * DK_D, (h + 1) * DK_D)
            o = _dot(scores, vh) + _dot(q_in[:, hs], s_prev)
            S_ref[h] = dec_col[hs, :] * s_prev + _dot_tn(k_out[:, hs], vh)
            o = o * lax.rsqrt(jnp.mean(o * o, axis=-1, keepdims=True) + EPS)
            o = o * ng[:, h * DV_D:(h + 1) * DV_D]
            y_ref[:, h * DV_D:(h + 1) * DV_D] = (_silu(r[:, h * DV_D:(h + 1) * DV_D]) * o).astype(y_ref.dtype)

    @pl.when(n == nch - 1)
    def _():
        sf_ref[...] = S_ref[...]


def _gla(proj, s0, ls, w2, b2, ng, l, *, B, L, row_off, out_dtype):
    C = math.gcd(L, CHUNK)
    c = math.gcd(L, GLA_SUB)
    nch = L // C
    r0 = row_off // C
    row = lambda b, n: r0 + b * nch + n
    par = lambda shape: pl.BlockSpec((None,) + shape, lambda b, n: (l,) + (0,) * len(shape))
    return pl.pallas_call(
        functools.partial(_gla_kernel, C=C, c=c, nch=nch),
        grid=(B, nch),
        in_specs=[pl.BlockSpec((C, 256), lambda b, n: (row(b, n), C_DQ)),
                  pl.BlockSpec((C, 256), lambda b, n: (row(b, n), C_DK)),
                  pl.BlockSpec((C, 512), lambda b, n: (row(b, n), C_DV)),
                  pl.BlockSpec((C, 512), lambda b, n: (row(b, n), C_DR)),
                  pl.BlockSpec((C, 128), lambda b, n: (row(b, n), C_SMALL)),
                  pl.BlockSpec((None, None, H_D, DK_D, DV_D), lambda b, n: (ls, b, 0, 0, 0)),
                  par((LANE, H_D * DK_D)), par((1, H_D * DK_D)), par((1, BW))],
        out_specs=[pl.BlockSpec((C, BW), lambda b, n: (b * nch + n, 0)),
                   pl.BlockSpec((None, H_D, DK_D, DV_D), lambda b, n: (b, 0, 0, 0))],
        out_shape=[jax.ShapeDtypeStruct((B * L, BW), out_dtype),
                   jax.ShapeDtypeStruct((B, H_D, DK_D, DV_D), f32)],
        scratch_shapes=[pltpu.VMEM((H_D, DK_D, DV_D), f32)],
        compiler_params=_cp("parallel", "arbitrary"),
        name="gla",
    )(proj, proj, proj, proj, proj, s0, w2, b2, ng)


def _fox_prep_kernel(sm_ref, bf_ref, lf_ref, c_ref, ct_ref, carry_ref, *, C):
    n = pl.program_id(1)

    @pl.when(n == 0)
    def _():
        carry_ref[...] = jnp.zeros_like(carry_ref)

    lf = _log_sigmoid(sm_ref[...] + bf_ref[...])
    lf = pltpu.roll(lf, LANE - H_C, axis=1)
    lane = lax.broadcasted_iota(jnp.int32, (C, LANE), 1)
    lf = jnp.where(lane < H_C, lf, 0.0)
    cs = carry_ref[...] + _cumsum_rows(lf)
    carry_ref[...] = cs[C - 1:C, :]
    lf_ref[...] = lf[:, 0:H_C]
    c_ref[...] = cs
    ct_ref[...] = cs.T[0:H_C, :]


def _fox_prep(proj, bf, l, *, B, L, row_off):
    C = math.gcd(L, CHUNK)
    nch = L // C
    r0 = row_off // C
    return pl.pallas_call(
        functools.partial(_fox_prep_kernel, C=C),
        grid=(B, nch),
        in_specs=[pl.BlockSpec((C, LANE), lambda b, n: (r0 + b * nch + n, C_SMALL)),
                  pl.BlockSpec((None, 1, LANE), lambda b, n: (l, 0, 0))],
        out_specs=[pl.BlockSpec((C, H_C), lambda b, n: (b * nch + n, 0)),
                   pl.BlockSpec((C, LANE), lambda b, n: (b * nch + n, 0)),
                   pl.BlockSpec((None, H_C, C), lambda b, n: (b, 0, n))],
        out_shape=[jax.ShapeDtypeStruct((B * L, H_C), f32),
                   jax.ShapeDtypeStruct((B * L, LANE), f32),
                   jax.ShapeDtypeStruct((B, H_C, L), f32)],
        scratch_shapes=[pltpu.VMEM((1, LANE), f32)],
        compiler_params=_cp("parallel", "arbitrary"),
        name="fox_prep",
    )(proj, bf)


def _fox_prompt_kernel(q_ref, k_ref, v_ref, c_ref, ct_ref, y_ref, kb_ref, vb_ref, *, tq):
    i = pl.program_id(1)

    @pl.when(i == 0)
    def _():
        kb_ref[...] = k_ref[...].astype(bf16)
        vb_ref[...] = v_ref[...].astype(bf16)

    q = (q_ref[...] * (DH_C ** -0.5)).astype(bf16)
    c = c_ref[...]
    causal = (lax.broadcasted_iota(jnp.int32, (tq, tq), 0)
              >= lax.broadcasted_iota(jnp.int32, (tq, tq), 1))
    for h in range(H_C):
        hs = slice(h * DH_C, (h + 1) * DH_C)
        qh = q[:, hs]
        ch = c[:, h:h + 1]

        def step(j, carry, masked):
            m, lsum, acc = carry
            start = pl.multiple_of(j * tq, tq)
            kj = kb_ref[pl.ds(start, tq), hs]
            vj = vb_ref[pl.ds(start, tq), hs]
            s = _dot_nt(qh, kj) + ch - ct_ref[h:h + 1, pl.ds(start, tq)]
            if masked:
                s = jnp.where(causal, s, -jnp.inf)
            m_new = jnp.maximum(m, jnp.max(s, axis=-1, keepdims=True))
            p = jnp.exp(s - m_new)
            alpha = jnp.exp(m - m_new)
            return (m_new, alpha * lsum + jnp.sum(p, axis=-1, keepdims=True),
                    alpha * acc + _dot(p, vj))

        init = (jnp.full((tq, 1), -jnp.inf, f32), jnp.zeros((tq, 1), f32), jnp.zeros((tq, DH_C), f32))
        carry = step(i, init, True)
        m, lsum, acc = lax.fori_loop(0, i, lambda j, cr: step(j, cr, False), carry)
        y_ref[:, hs] = (acc / lsum).astype(y_ref.dtype)


def _fox_prompt(proj, c, ct, *, B, L, out_dtype):
    tq = _pick(L, (256, 128, 64, 32, 16, 8))
    nq = L // tq
    return pl.pallas_call(
        functools.partial(_fox_prompt_kernel, tq=tq),
        grid=(B, nq),
        in_specs=[pl.BlockSpec((tq, BW), lambda b, i: (b * nq + i, C_CQ)),
                  pl.BlockSpec((L, BW), lambda b, i: (b, C_CK)),
                  pl.BlockSpec((L, BW), lambda b, i: (b, C_CV)),
                  pl.BlockSpec((tq, LANE), lambda b, i: (b * nq + i, 0)),
                  pl.BlockSpec((None, H_C, L), lambda b, i: (b, 0, 0))],
        out_specs=pl.BlockSpec((tq, BW), lambda b, i: (b * nq + i, 0)),
        out_shape=jax.ShapeDtypeStruct((B * L, BW), out_dtype),
        scratch_shapes=[pltpu.VMEM((L, BW), bf16), pltpu.VMEM((L, BW), bf16)],
        compiler_params=_cp("parallel", "arbitrary"),
        name="fox_prompt",
    )(proj, proj, proj, c, ct)


def _fox_sample_kernel(pt_ref, q_ref, kn_ref, vn_ref, cn_ref, kp_ref, vp_ref, lfp_ref, y_ref,
                       qbd_ref, m_ref, l_ref, acc_ref, carry_ref, *, T, npages):
    j = pl.program_id(1)
    R = H_C * T
    row_head = lax.broadcasted_iota(jnp.int32, (R, BW), 0) // T
    lane_head = lax.broadcasted_iota(jnp.int32, (R, BW), 1) // DH_C
    head_mask = row_head == lane_head

    def rep_rows(x):
        return jnp.broadcast_to(x[:, None, :], (H_C, T, x.shape[-1])).reshape(R, x.shape[-1])

    @pl.when(j == 0)
    def _():
        q = q_ref[...] * (DH_C ** -0.5)
        qbd = jnp.where(head_mask, jnp.broadcast_to(q[None], (H_C, T, BW)).reshape(R, BW), 0.0)
        qbd_ref[...] = qbd.astype(bf16)
        cn_t = cn_ref[...].T[0:H_C, :]
        s = _dot_nt(qbd, kn_ref[...]) - rep_rows(cn_t)
        t_of_row = lax.broadcasted_iota(jnp.int32, (R, T), 0) % T
        s = jnp.where(lax.broadcasted_iota(jnp.int32, (R, T), 1) <= t_of_row, s, -jnp.inf)
        m = jnp.max(s, axis=-1, keepdims=True)
        p = jnp.exp(s - m)
        m_ref[...] = m
        l_ref[...] = jnp.sum(p, axis=-1, keepdims=True)
        acc_ref[...] = _dot(p, vn_ref[...])
        carry_ref[...] = jnp.zeros_like(carry_ref)

    lf_t = lfp_ref[...].T
    strict = (lax.broadcasted_iota(jnp.int32, (PAGE_SIZE, PAGE_SIZE), 0)
              > lax.broadcasted_iota(jnp.int32, (PAGE_SIZE, PAGE_SIZE), 1)).astype(f32)
    suffix = jnp.dot(lf_t, strict, precision=lax.Precision.HIGHEST, preferred_element_type=f32)
    bias = carry_ref[...] + suffix
    carry_ref[...] = carry_ref[...] + jnp.sum(lf_t, axis=-1, keepdims=True)
    s = _dot_nt(qbd_ref[...], kp_ref[...]) + rep_rows(bias)
    m_old = m_ref[...]
    m_new = jnp.maximum(m_old, jnp.max(s, axis=-1, keepdims=True))
    p = jnp.exp(s - m_new)
    alpha = jnp.exp(m_old - m_new)
    m_ref[...] = m_new
    l_ref[...] = alpha * l_ref[...] + jnp.sum(p, axis=-1, keepdims=True)
    acc_ref[...] = alpha * acc_ref[...] + _dot(p, vp_ref[...])

    @pl.when(j == npages - 1)
    def _():
        o = jnp.where(head_mask, acc_ref[...] / l_ref[...], 0.0)
        y_ref[...] = jnp.sum(o.reshape(H_C, T, BW), axis=0).astype(y_ref.dtype)


def _fox_sample(proj, cnew, cache_k, cache_v, cache_lf, page_table, l, *, B, T, row_off, out_dtype):
    npages = page_table.shape[1]
    r0 = row_off // T
    page = lambda b, j, pt: pt[b, npages - 1 - j]
    grid_spec = pltpu.PrefetchScalarGridSpec(
        num_scalar_prefetch=1,
        grid=(B, npages),
        in_specs=[pl.BlockSpec((T, BW), lambda b, j, pt: (r0 + b, C_CQ)),
                  pl.BlockSpec((T, BW), lambda b, j, pt: (r0 + b, C_CK)),
                  pl.BlockSpec((T, BW), lambda b, j, pt: (r0 + b, C_CV)),
                  pl.BlockSpec((T, LANE), lambda b, j, pt: (b, 0)),
                  pl.BlockSpec((None, None, PAGE_SIZE, BW), lambda b, j, pt: (l, page(b, j, pt), 0, 0)),
                  pl.BlockSpec((None, None, PAGE_SIZE, BW), lambda b, j, pt: (l, page(b, j, pt), 0, 0)),
                  pl.BlockSpec((None, None, PAGE_SIZE, H_C), lambda b, j, pt: (l, page(b, j, pt), 0, 0))],
        out_specs=pl.BlockSpec((T, BW), lambda b, j, pt: (b, 0)),
        scratch_shapes=[pltpu.VMEM((H_C * T, BW), bf16), pltpu.VMEM((H_C * T, 1), f32),
                        pltpu.VMEM((H_C * T, 1), f32), pltpu.VMEM((H_C * T, BW), f32),
                        pltpu.VMEM((H_C, 1), f32)],
    )
    return pl.pallas_call(
        functools.partial(_fox_sample_kernel, T=T, npages=npages),
        grid_spec=grid_spec,
        out_shape=jax.ShapeDtypeStruct((B * T, BW), out_dtype),
        compiler_params=_cp("parallel", "arbitrary"),
        name="fox_sample",
    )(page_table, proj, proj, proj, cnew, cache_k, cache_v, cache_lf)


def _xattn_kernel(q_ref, k_ref, v_ref, o_ref):
    q = q_ref[...]
    dh = q.shape[-1] // H_X
    for h in range(H_X):
        hs = slice(h * dh, (h + 1) * dh)
        s = _dot_nt(q[:, hs], k_ref[:, hs]) * (dh ** -0.5)
        m = jnp.max(s, axis=-1, keepdims=True)
        p = jnp.exp(s - m)
        p = p / jnp.sum(p, axis=-1, keepdims=True)
        o_ref[:, hs] = _dot(p, v_ref[:, hs]).astype(o_ref.dtype)


def _xattn(xq, mem_k, mem_v, lk, lv, *, B, L, row_off, out_dtype):
    D = xq.shape[1]
    n_mem = mem_k.shape[2]
    tq = _pick(L, (512, 256, 128, 64, 32, 16, 8))
    nq = L // tq
    r0 = row_off // tq
    return pl.pallas_call(
        _xattn_kernel,
        grid=(B, nq),
        in_specs=[pl.BlockSpec((tq, D), lambda b, i: (r0 + b * nq + i, 0)),
                  pl.BlockSpec((None, None, n_mem, D), lambda b, i: (lk, b, 0, 0)),
                  pl.BlockSpec((None, None, n_mem, D), lambda b, i: (lv, b, 0, 0))],
        out_specs=pl.BlockSpec((tq, D), lambda b, i: (b * nq + i, 0)),
        out_shape=jax.ShapeDtypeStruct((B * L, D), out_dtype),
        compiler_params=_cp("parallel", "arbitrary"),
        name="xattn_core",
    )(xq, mem_k, mem_v)


def _rope_tables(pos):
    half = DK_A // 2
    inv_freq = 1.0 / (ROPE_BASE ** (jnp.arange(half, dtype=f32) / half))
    ang = pos.astype(f32)[:, None] * inv_freq[None, :]
    cos, sin = jnp.cos(ang), jnp.sin(ang)
    return (jnp.tile(jnp.concatenate([cos, cos], axis=1), (1, H_A)),
            jnp.tile(jnp.concatenate([-sin, sin], axis=1), (1, H_A)))


def _permute_w_in(w_in):
    sizes = (256, 256, 512, 512, 512, CONV_DIM, H_B, 512, 512, 512, H_C, 256, 256, 512, 512, GLA_RANK)
    offs = [0]
    for s in sizes:
        offs.append(offs[-1] + s)
    (a_q, a_k, a_v, a_g, b_z, b_xbc, b_dt, c_q, c_k, c_v, c_f, d_q, d_k, d_v, d_r, d_lr) = (
        w_in[:, :, offs[i]:offs[i + 1]] for i in range(len(sizes)))
    d = w_in.shape[:2]
    used = 9 * 512 + 5 * 256 + H_B + H_C + GLA_RANK
    cols = [a_v, a_g, b_z, b_xbc[:, :, :BW], c_q, c_k, c_v, d_v, d_r,
            a_q, a_k, b_xbc[:, :, BW:], d_q, d_k,
            b_dt, c_f, d_lr, jnp.zeros(d + (NP_COLS - used,), w_in.dtype)]
    w_mix = jnp.concatenate(cols, axis=2).astype(bf16)
    w_gate = w_in[:, :, offs[-1]:].astype(bf16)
    return w_mix, w_gate


def _pad_lanes(x, start):
    depth, n = x.shape
    return jnp.zeros((depth, 1, LANE), f32).at[:, 0, start:start + n].set(x)


def kernel(x_prompt, x_sample, cache_fox_k, cache_fox_v, cache_fox_logf, cache_mem_k, cache_mem_v, state_ret, state_ssm, state_conv, state_gla, page_table, mem_prompt, w_in, b_gate, ret_norm_g, ssd_conv_w, ssd_conv_b, ssd_dt_bias, ssd_a_log, ssd_d, ssd_norm_g, fox_bf, gla_w_lr2, gla_b_lr, gla_norm_g, w_branch, w_out, norm_mix_g, norm_x_g, norm_mem_g, w_xq, w_mem_kv, w_xo, norm_ffn_g, w_ffn_up, w_ffn_down, norm_final_g):
    Bp, Lp, D = x_prompt.shape
    Bs, Ls, _ = x_sample.shape
    depth = w_in.shape[0]
    n_mem = mem_prompt.shape[1]
    past_len = page_table.shape[1] * PAGE_SIZE
    Tp, Ts = Bp * Lp, Bs * Ls

    w_mix, w_gate = _permute_w_in(w_in)
    w_branch_b = w_branch.astype(bf16)
    w_out_b = w_out.astype(bf16)
    w_xq_b = w_xq.astype(bf16)
    w_xo_b = w_xo.astype(bf16)
    w_kv_b = w_mem_kv.astype(bf16)
    w_up_b = w_ffn_up.astype(bf16)
    w_down_b = w_ffn_down.astype(bf16)
    b_gate4 = b_gate.reshape(depth, N_BRANCH, 1, D)
    g_mix = norm_mix_g.reshape(depth, 1, D)
    g_x = norm_x_g.reshape(depth, 1, D)
    g_mem = norm_mem_g.reshape(depth, 1, D)
    g_ffn = norm_ffn_g.reshape(depth, 1, D)
    ret_g = ret_norm_g.reshape(depth, 1, BW)
    gla_g = gla_norm_g.reshape(depth, 1, BW)
    ssd_g = ssd_norm_g.reshape(depth, 1, BW)
    conv_b = ssd_conv_b.reshape(depth, 1, CONV_DIM)
    dt_bias = _pad_lanes(ssd_dt_bias, 0)
    a_log = _pad_lanes(ssd_a_log, 0)
    d_skip = jnp.repeat(ssd_d, P_B, axis=1).reshape(depth, 1, BW)
    fox_b = _pad_lanes(fox_bf, H_B)
    w_lr2 = jnp.zeros((depth, LANE, H_D * DK_D), f32).at[:, H_B + H_C:H_B + H_C + GLA_RANK, :].set(
        gla_w_lr2).astype(bf16)
    b_lr = gla_b_lr.reshape(depth, 1, H_D * DK_D)
    cos_p, sin_p = _rope_tables(jnp.arange(Lp, dtype=jnp.int32))
    cos_s, sin_s = _rope_tables(past_len + jnp.arange(Ls, dtype=jnp.int32))
    ck = cache_fox_k.reshape(cache_fox_k.shape[:3] + (BW,))
    cv = cache_fox_v.reshape(cache_fox_v.shape[:3] + (BW,))
    cmk = cache_mem_k.reshape(cache_mem_k.shape[:3] + (D,))
    cmv = cache_mem_v.reshape(cache_mem_v.shape[:3] + (D,))
    z_ret = jnp.zeros((1, Bp, H_A, DK_A, DV_A), f32)
    z_ssm = jnp.zeros((1, Bp, H_B, N_B, P_B), f32)
    z_conv = jnp.zeros((1, Bp, CONV_W - 1, CONV_DIM), f32)
    z_gla = jnp.zeros((1, Bp, H_D, DK_D, DV_D), f32)
    mem2d = mem_prompt.reshape(Bp * n_mem, D)

    h = jnp.concatenate([x_prompt.reshape(Tp, D), x_sample.reshape(Ts, D)], axis=0)
    outs = {k: [] for k in ("ret_p", "ssm_p", "conv_p", "gla_p", "fk_p", "fv_p", "flf_p", "mk_p", "mv_p",
                            "ret_s", "ssm_s", "conv_s", "gla_s", "fk_s", "fv_s", "flf_s")}
    for l in range(depth):
        proj = _norm_matmul(h, g_mix, w_mix, l, tn=1024)

        pk = dict(B=Bp, L=Lp, row_off=0, out_dtype=bf16)
        sk = dict(B=Bs, L=Ls, row_off=Tp, out_dtype=f32)
        ya_p, ret_p = _retention(proj, cos_p, sin_p, z_ret, 0, ret_g, l, **pk)
        ya_s, ret_s = _retention(proj, cos_s, sin_s, state_ret, l, ret_g, l, **sk)
        yb_p, conv_p, ssm_p = _ssd(proj, z_conv, z_ssm, 0, ssd_conv_w, conv_b, dt_bias, a_log, d_skip,
                                   ssd_g, l, **pk)
        yb_s, conv_s, ssm_s = _ssd(proj, state_conv, state_ssm, l, ssd_conv_w, conv_b, dt_bias, a_log,
                                   d_skip, ssd_g, l, **sk)
        lf_p, c_p, ct_p = _fox_prep(proj, fox_b, l, B=Bp, L=Lp, row_off=0)
        lf_s, c_s, _ = _fox_prep(proj, fox_b, l, B=Bs, L=Ls, row_off=Tp)
        yc_p = _fox_prompt(proj, c_p, ct_p, B=Bp, L=Lp, out_dtype=bf16)
        yc_s = _fox_sample(proj, c_s, ck, cv, cache_fox_logf, page_table, l, B=Bs, T=Ls, row_off=Tp,
                           out_dtype=f32)
        yd_p, gla_p = _gla(proj, z_gla, 0, w_lr2, b_lr, gla_g, l, **pk)
        yd_s, gla_s = _gla(proj, state_gla, l, w_lr2, b_lr, gla_g, l, **sk)

        h = _merge(h, g_mix, (ya_p, yb_p, yc_p, yd_p), (ya_s, yb_s, yc_s, yd_s),
                   w_gate, b_gate4, w_branch_b, w_out_b, l)

        kv = _norm_matmul(mem2d, g_mem, w_kv_b, l, tn=D, split=True)
        mem_kv = kv.reshape(2, Bp, n_mem, D)
        xq = _norm_matmul(h, g_x, w_xq_b, l, tn=D)
        o_p = _xattn(xq, mem_kv, mem_kv, 0, 1, B=Bp, L=Lp, row_off=0, out_dtype=bf16)
        o_s = _xattn(xq, cmk, cmv, l, l, B=Bs, L=Ls, row_off=Tp, out_dtype=f32)
        h = _xo(h, o_p, o_s, w_xo_b, l)
        h = _ffn(h, g_ffn, w_up_b, w_down_b, l)

        ckv = proj[:, C_CK * BW:(C_CV + 1) * BW]
        outs["ret_p"].append(ret_p); outs["ssm_p"].append(ssm_p); outs["conv_p"].append(conv_p)
        outs["gla_p"].append(gla_p)
        outs["fk_p"].append(ckv[:Tp, :BW].reshape(Bp, Lp, H_C, DH_C))
        outs["fv_p"].append(ckv[:Tp, BW:].reshape(Bp, Lp, H_C, DH_C))
        outs["flf_p"].append(lf_p.reshape(Bp, Lp, H_C))
        outs["mk_p"].append(kv[0].reshape(Bp, n_mem, H_X, D // H_X))
        outs["mv_p"].append(kv[1].reshape(Bp, n_mem, H_X, D // H_X))
        outs["ret_s"].append(ret_s); outs["ssm_s"].append(ssm_s); outs["conv_s"].append(conv_s)
        outs["gla_s"].append(gla_s)
        outs["fk_s"].append(ckv[Tp:, :BW].reshape(Bs, Ls, H_C, DH_C))
        outs["fv_s"].append(ckv[Tp:, BW:].reshape(Bs, Ls, H_C, DH_C))
        outs["flf_s"].append(lf_s.reshape(Bs, Ls, H_C))

    y = _final_norm(h, norm_final_g.reshape(1, D))
    st = {k: jnp.stack(v, axis=0) for k, v in outs.items()}
    return (y[:Tp].reshape(Bp, Lp, D), y[Tp:].reshape(Bs, Ls, D),
            st["ret_p"], st["ssm_p"], st["conv_p"], st["gla_p"], st["fk_p"], st["fv_p"], st["flf_p"],
            st["mk_p"], st["mv_p"],
            st["ret_s"], st["ssm_s"], st["conv_s"], st["gla_s"], st["fk_s"], st["fv_s"], st["flf_s"])
```

```python
import functools
import math

import jax
import jax.numpy as jnp
from jax import lax
from jax.experimental import pallas as pl
from jax.experimental.pallas import tpu as pltpu

f32 = jnp.float32
bf16 = jnp.bfloat16

EPS = 1e-6
CHUNK = 128
GLA_CHUNK = 16
GLA_SUB = 16
ROPE_BASE = 10000.0
PAGE_SIZE = 128
H_A, DK_A, DV_A = 4, 64, 128
H_B, P_B, N_B, G_B, CONV_W = 8, 64, 64, 2, 4
H_C, DH_C = 8, 64
H_D, DK_D, DV_D = 4, 64, 128
GLA_RANK = 16
GLA_TAU = 16.0
N_BRANCH = 4
H_X = 4
BW = 512
CONV_DIM = BW + 2 * G_B * N_B

C_AV, C_AG, C_BZ, C_BXS, C_CQ, C_CK, C_CV, C_DV, C_DR = range(9)
C_AQ, C_AK, C_BBC, C_DQ, C_DK = range(18, 23)
C_SMALL = 46
NP_COLS = 6144
LANE = 128
VMEM_LIMIT = 56 * 1024 * 1024


def _cp(*sem):
    return pltpu.CompilerParams(dimension_semantics=sem, vmem_limit_bytes=VMEM_LIMIT)


def _pick(n, cands):
    for c in cands:
        if n % c == 0:
            return c
    raise ValueError(f"no tile for {n}")


def _dot(a, b):
    return jnp.dot(a.astype(bf16), b.astype(bf16), preferred_element_type=f32)


def _dot_nt(a, b):
    return lax.dot_general(a.astype(bf16), b.astype(bf16), (((1,), (1,)), ((), ())),
                           preferred_element_type=f32)


def _dot_tn(a, b):
    return lax.dot_general(a.astype(bf16), b.astype(bf16), (((0,), (0,)), ((), ())),
                           preferred_element_type=f32)


def _silu(x):
    return x * jax.nn.sigmoid(x)


def _log_sigmoid(x):
    return jnp.minimum(x, 0.0) - jnp.log1p(jnp.exp(-jnp.abs(x)))


def _softplus(x):
    return jnp.maximum(x, 0.0) + jnp.log1p(jnp.exp(-jnp.abs(x)))


def _cumsum_rows(x):
    n = x.shape[0]
    row = lax.broadcasted_iota(jnp.int32, x.shape, 0)
    k = 1
    while k < n:
        x = x + jnp.where(row >= k, pltpu.roll(x, k, axis=0), 0.0)
        k *= 2
    return x


def _rms(x, g):
    return x * lax.rsqrt(jnp.mean(x * x, axis=-1, keepdims=True) + EPS) * g


def _norm_matmul_kernel(x_ref, g_ref, w_ref, o_ref, xn_ref):
    @pl.when(pl.program_id(1) == 0)
    def _():
        xn_ref[...] = _rms(x_ref[...], g_ref[...]).astype(bf16)

    o_ref[...] = jnp.dot(xn_ref[...], w_ref[...], preferred_element_type=f32).astype(o_ref.dtype)


def _norm_matmul(x, g, w, l, *, tn, out_dtype=f32, split=False):
    M, K = x.shape
    N = w.shape[2]
    tm = _pick(M, (1024, 512, 256, 128, 64, 32, 16, 8))
    if split:
        out_shape = jax.ShapeDtypeStruct((N // tn, M, tn), out_dtype)
        out_spec = pl.BlockSpec((None, tm, tn), lambda i, j: (j, i, 0))
    else:
        out_shape = jax.ShapeDtypeStruct((M, N), out_dtype)
        out_spec = pl.BlockSpec((tm, tn), lambda i, j: (i, j))
    return pl.pallas_call(
        _norm_matmul_kernel,
        grid=(M // tm, N // tn),
        in_specs=[pl.BlockSpec((tm, K), lambda i, j: (i, 0)),
                  pl.BlockSpec((None, 1, K), lambda i, j: (l, 0, 0)),
                  pl.BlockSpec((None, K, tn), lambda i, j: (l, 0, j))],
        out_specs=out_spec,
        out_shape=out_shape,
        scratch_shapes=[pltpu.VMEM((tm, K), bf16)],
        compiler_params=_cp("parallel", "arbitrary"),
        name="norm_matmul",
    )(x, g, w)


def _two_src(tm, width, n_p, nargs):
    if nargs == 1:
        return (pl.BlockSpec((tm, width), lambda m: (jnp.minimum(m, n_p - 1), 0)),
                pl.BlockSpec((tm, width), lambda m: (jnp.maximum(m - n_p, 0), 0)))
    return (pl.BlockSpec((tm, width), lambda m, i: (jnp.minimum(m, n_p - 1), 0)),
            pl.BlockSpec((tm, width), lambda m, i: (jnp.maximum(m - n_p, 0), 0)))


def _merge_kernel(h_ref, g_ref, *refs, n_p):
    yp = refs[0:4]
    ys = refs[4:8]
    wg_ref, bg_ref, wb_ref, wo_ref, o_ref, xn_ref, acc_ref = refs[8:]
    m = pl.program_id(0)
    i = pl.program_id(1)

    @pl.when(i == 0)
    def _():
        xn_ref[...] = _rms(h_ref[...], g_ref[...]).astype(bf16)
        acc_ref[...] = jnp.zeros_like(acc_ref)

    gate = jax.nn.sigmoid(jnp.dot(xn_ref[...], wg_ref[...], preferred_element_type=f32) + bg_ref[...])
    for k in range(N_BRANCH):
        @pl.when(i == k)
        def _(k=k):
            y = jnp.where(m < n_p, yp[k][...].astype(bf16), ys[k][...].astype(bf16))
            acc_ref[...] += gate * jnp.dot(y, wb_ref[...], preferred_element_type=f32)

    @pl.when(i == N_BRANCH - 1)
    def _():
        o_ref[...] = h_ref[...] + jnp.dot(acc_ref[...].astype(bf16), wo_ref[...],
                                          preferred_element_type=f32)


def _merge(h, g, yps, yss, w_gate, b_gate, w_branch, w_out, l):
    T, D = h.shape
    Tp, Ts = yps[0].shape[0], yss[0].shape[0]
    tm = _pick(math.gcd(Tp, Ts), (512, 256, 128, 64, 32, 16, 8))
    n_p = Tp // tm
    sp, ss = _two_src(tm, BW, n_p, 2)
    return pl.pallas_call(
        functools.partial(_merge_kernel, n_p=n_p),
        grid=(T // tm, N_BRANCH),
        in_specs=[pl.BlockSpec((tm, D), lambda m, i: (m, 0)),
                  pl.BlockSpec((None, 1, D), lambda m, i: (l, 0, 0))]
                 + [sp] * 4 + [ss] * 4
                 + [pl.BlockSpec((None, D, D), lambda m, i: (l, 0, i)),
                    pl.BlockSpec((None, None, 1, D), lambda m, i: (l, i, 0, 0)),
                    pl.BlockSpec((None, None, BW, D), lambda m, i: (l, i, 0, 0)),
                    pl.BlockSpec((None, D, D), lambda m, i: (l, 0, 0))],
        out_specs=pl.BlockSpec((tm, D), lambda m, i: (m, 0)),
        out_shape=jax.ShapeDtypeStruct((T, D), f32),
        scratch_shapes=[pltpu.VMEM((tm, D), bf16), pltpu.VMEM((tm, D), f32)],
        compiler_params=_cp("parallel", "arbitrary"),
        name="branch_merge",
    )(h, g, *yps, *yss, w_gate, b_gate, w_branch, w_out)


def _xo_kernel(h_ref, op_ref, os_ref, w_ref, o_ref, *, n_p):
    m = pl.program_id(0)
    o = jnp.where(m < n_p, op_ref[...].astype(bf16), os_ref[...].astype(bf16))
    o_ref[...] = h_ref[...] + jnp.dot(o, w_ref[...], preferred_element_type=f32)


def _xo(h, o_p, o_s, w, l):
    T, D = h.shape
    Tp, Ts = o_p.shape[0], o_s.shape[0]
    tm = _pick(math.gcd(Tp, Ts), (1024, 512, 256, 128, 64, 32, 16, 8))
    n_p = Tp // tm
    sp, ss = _two_src(tm, D, n_p, 1)
    return pl.pallas_call(
        functools.partial(_xo_kernel, n_p=n_p),
        grid=(T // tm,),
        in_specs=[pl.BlockSpec((tm, D), lambda m: (m, 0)), sp, ss,
                  pl.BlockSpec((None, D, D), lambda m: (l, 0, 0))],
        out_specs=pl.BlockSpec((tm, D), lambda m: (m, 0)),
        out_shape=jax.ShapeDtypeStruct((T, D), f32),
        compiler_params=_cp("parallel"),
        name="xattn_out",
    )(h, o_p, o_s, w)


def _ffn_kernel(h_ref, g_ref, wa_ref, wb_ref, wd_ref, o_ref, xn_ref, acc_ref, *, nj):
    j = pl.program_id(1)

    @pl.when(j == 0)
    def _():
        xn_ref[...] = _rms(h_ref[...], g_ref[...]).astype(bf16)
        acc_ref[...] = jnp.zeros_like(acc_ref)

    xn = xn_ref[...]
    a = jnp.dot(xn, wa_ref[...], preferred_element_type=f32)
    b = jnp.dot(xn, wb_ref[...], preferred_element_type=f32)
    acc_ref[...] += jnp.dot((_silu(a) * b).astype(bf16), wd_ref[...], preferred_element_type=f32)

    @pl.when(j == nj - 1)
    def _():
        o_ref[...] = h_ref[...] + acc_ref[...]


def _ffn(h, g, w_up, w_down, l):
    T, D = h.shape
    d_ff = w_down.shape[1]
    tm = _pick(T, (1024, 512, 256, 128, 64, 32, 16, 8))
    tf = _pick(d_ff, (256, 128))
    nj = d_ff // tf
    return pl.pallas_call(
        functools.partial(_ffn_kernel, nj=nj),
        grid=(T // tm, nj),
        in_specs=[pl.BlockSpec((tm, D), lambda m, j: (m, 0)),
                  pl.BlockSpec((None, 1, D), lambda m, j: (l, 0, 0)),
                  pl.BlockSpec((None, D, tf), lambda m, j: (l, 0, j)),
                  pl.BlockSpec((None, D, tf), lambda m, j: (l, 0, j + nj)),
                  pl.BlockSpec((None, tf, D), lambda m, j: (l, j, 0))],
        out_specs=pl.BlockSpec((tm, D), lambda m, j: (m, 0)),
        out_shape=jax.ShapeDtypeStruct((T, D), f32),
        scratch_shapes=[pltpu.VMEM((tm, D), bf16), pltpu.VMEM((tm, D), f32)],
        compiler_params=_cp("parallel", "arbitrary"),
        name="swiglu_ffn",
    )(h, g, w_up, w_up, w_down)


def _final_norm_kernel(x_ref, g_ref, o_ref):
    o_ref[...] = _rms(x_ref[...], g_ref[...])


def _final_norm(h, g):
    T, D = h.shape
    tm = _pick(T, (1024, 512, 256, 128, 64, 32, 16, 8))
    return pl.pallas_call(
        _final_norm_kernel,
        grid=(T // tm,),
        in_specs=[pl.BlockSpec((tm, D), lambda m: (m, 0)), pl.BlockSpec((1, D), lambda m: (0, 0))],
        out_specs=pl.BlockSpec((tm, D), lambda m: (m, 0)),
        out_shape=jax.ShapeDtypeStruct((T, D), f32),
        compiler_params=_cp("parallel"),
        name="final_norm",
    )(h, g)


def _ret_kernel(q_ref, k_ref, v_ref, g_ref, cos_ref, sin_ref, s0_ref, ng_ref, y_ref, sf_ref, S_ref,
                *, C, nch):
    n = pl.program_id(1)

    @pl.when(n == 0)
    def _():
        S_ref[...] = s0_ref[...]

    cos = cos_ref[...]
    sin = sin_ref[...]
    lane = lax.broadcasted_iota(jnp.int32, (C, H_A * DK_A), 1)
    first_half = (lane % DK_A) < (DK_A // 2)

    def rot(x):
        fwd = pltpu.roll(x, DK_A // 2, axis=1)
        bwd = pltpu.roll(x, H_A * DK_A - DK_A // 2, axis=1)
        return x * cos + jnp.where(first_half, bwd, fwd) * sin

    q = rot(q_ref[...])
    k = rot(k_ref[...]) * (DK_A ** -0.5)
    v = v_ref[...]
    gate = g_ref[...]
    ng = ng_ref[...]
    t_col = lax.broadcasted_iota(jnp.int32, (C, 1), 0).astype(f32)
    diff = (lax.broadcasted_iota(jnp.int32, (C, C), 0)
            - lax.broadcasted_iota(jnp.int32, (C, C), 1))
    for h in range(H_A):
        lg = math.log1p(-2.0 ** (-5.0 - h))
        qh = q[:, h * DK_A:(h + 1) * DK_A]
        kh = k[:, h * DK_A:(h + 1) * DK_A]
        vh = v[:, h * DV_A:(h + 1) * DV_A]
        decay = jnp.where(diff >= 0, jnp.exp(diff.astype(f32) * lg), 0.0)
        scores = _dot_nt(qh, kh) * decay
        s_prev = S_ref[h]
        o = _dot(scores, vh) + _dot(qh * jnp.exp((t_col + 1.0) * lg), s_prev)
        k_out = kh * jnp.exp((C - 1.0 - t_col) * lg)
        S_ref[h] = math.exp(C * lg) * s_prev + _dot_tn(k_out, vh)
        o = o - jnp.mean(o, axis=-1, keepdims=True)
        o = o * lax.rsqrt(jnp.mean(o * o, axis=-1, keepdims=True) + EPS)
        o = o * ng[:, h * DV_A:(h + 1) * DV_A]
        y_ref[:, h * DV_A:(h + 1) * DV_A] = (_silu(gate[:, h * DV_A:(h + 1) * DV_A]) * o).astype(y_ref.dtype)

    @pl.when(n == nch - 1)
    def _():
        sf_ref[...] = S_ref[...]


def _retention(proj, cos, sin, s0, ls, ng, l, *, B, L, row_off, out_dtype):
    C = math.gcd(L, CHUNK)
    nch = L // C
    r0 = row_off // C
    row = lambda b, n: r0 + b * nch + n
    return pl.pallas_call(
        functools.partial(_ret_kernel, C=C, nch=nch),
        grid=(B, nch),
        in_specs=[pl.BlockSpec((C, 256), lambda b, n: (row(b, n), C_AQ)),
                  pl.BlockSpec((C, 256), lambda b, n: (row(b, n), C_AK)),
                  pl.BlockSpec((C, 512), lambda b, n: (row(b, n), C_AV)),
                  pl.BlockSpec((C, 512), lambda b, n: (row(b, n), C_AG)),
                  pl.BlockSpec((C, 256), lambda b, n: (n, 0)),
                  pl.BlockSpec((C, 256), lambda b, n: (n, 0)),
                  pl.BlockSpec((None, None, H_A, DK_A, DV_A), lambda b, n: (ls, b, 0, 0, 0)),
                  pl.BlockSpec((None, 1, BW), lambda b, n: (l, 0, 0))],
        out_specs=[pl.BlockSpec((C, BW), lambda b, n: (b * nch + n, 0)),
                   pl.BlockSpec((None, H_A, DK_A, DV_A), lambda b, n: (b, 0, 0, 0))],
        out_shape=[jax.ShapeDtypeStruct((B * L, BW), out_dtype),
                   jax.ShapeDtypeStruct((B, H_A, DK_A, DV_A), f32)],
        scratch_shapes=[pltpu.VMEM((H_A, DK_A, DV_A), f32)],
        compiler_params=_cp("parallel", "arbitrary"),
        name="retention",
    )(proj, proj, proj, proj, cos, sin, s0, ng)


def _ssd_kernel(z_ref, xs_ref, bc_ref, sm_ref, c0_ref, s0_ref, cw_ref, cb_ref, dtb_ref, al_ref,
                dsk_ref, ng_ref, y_ref, cn_ref, sf_ref, xpad_ref, S_ref, o_ref, *, C, nch):
    n = pl.program_id(1)
    P0 = 8 - (CONV_W - 1)

    @pl.when(n == 0)
    def _():
        xpad_ref[P0:8, :] = c0_ref[...]
        S_ref[...] = s0_ref[...]

    xpad_ref[8:8 + C, 0:BW] = xs_ref[...]
    xpad_ref[8:8 + C, BW:CONV_DIM] = bc_ref[...]
    cw = cw_ref[...]
    conv = cb_ref[...] + xpad_ref[P0:P0 + C, :] * cw[0:1, :]
    for j in range(1, CONV_W):
        conv = conv + xpad_ref[P0 + j:P0 + j + C, :] * cw[j:j + 1, :]
    tail = xpad_ref[C + P0:C + 8, :]
    cn_ref[...] = tail
    xpad_ref[P0:8, :] = tail

    u = _silu(conv)
    xs = u[:, 0:BW]
    dt = _softplus(sm_ref[...] + dtb_ref[...])
    la = dt * (-jnp.exp(al_ref[...]))
    cum = _cumsum_rows(la)
    cum_t = cum.T
    cum_last = cum[C - 1:C, :]
    causal = (lax.broadcasted_iota(jnp.int32, (C, C), 0)
              >= lax.broadcasted_iota(jnp.int32, (C, C), 1))
    dsk = dsk_ref[...]
    rep = H_B // G_B
    for g in range(G_B):
        bm = u[:, BW + g * N_B:BW + (g + 1) * N_B]
        cm = u[:, BW + G_B * N_B + g * N_B:BW + G_B * N_B + (g + 1) * N_B]
        scores_g = _dot_nt(cm, bm)
        for h in range(g * rep, (g + 1) * rep):
            cum_h = cum[:, h:h + 1]
            seg = cum_h - cum_t[h:h + 1, :]
            decay = jnp.exp(jnp.where(causal, seg, -jnp.inf))
            xh = xs[:, h * P_B:(h + 1) * P_B]
            vh = xh * dt[:, h:h + 1]
            s_prev = S_ref[h]
            o = _dot(scores_g * decay, vh) + _dot(cm * jnp.exp(cum_h), s_prev)
            k_out = bm * jnp.exp(cum_last[:, h:h + 1] - cum_h)
            S_ref[h] = jnp.exp(cum_last[:, h:h + 1]) * s_prev + _dot_tn(k_out, vh)
            o_ref[:, h * P_B:(h + 1) * P_B] = o + dsk[:, h * P_B:(h + 1) * P_B] * xh
    y = o_ref[...] * _silu(z_ref[...])
    y_ref[...] = _rms(y, ng_ref[...]).astype(y_ref.dtype)

    @pl.when(n == nch - 1)
    def _():
        sf_ref[...] = S_ref[...]


def _ssd(proj, c0, s0, ls, cw, cb, dtb, al, dsk, ng, l, *, B, L, row_off, out_dtype):
    C = math.gcd(L, CHUNK)
    nch = L // C
    r0 = row_off // C
    row = lambda b, n: r0 + b * nch + n
    par = lambda shape: pl.BlockSpec((None,) + shape, lambda b, n: (l,) + (0,) * len(shape))
    return pl.pallas_call(
        functools.partial(_ssd_kernel, C=C, nch=nch),
        grid=(B, nch),
        in_specs=[pl.BlockSpec((C, 512), lambda b, n: (row(b, n), C_BZ)),
                  pl.BlockSpec((C, 512), lambda b, n: (row(b, n), C_BXS)),
                  pl.BlockSpec((C, 256), lambda b, n: (row(b, n), C_BBC)),
                  pl.BlockSpec((C, 128), lambda b, n: (row(b, n), C_SMALL)),
                  pl.BlockSpec((None, None, CONV_W - 1, CONV_DIM), lambda b, n: (ls, b, 0, 0)),
                  pl.BlockSpec((None, None, H_B, N_B, P_B), lambda b, n: (ls, b, 0, 0, 0)),
                  par((CONV_W, CONV_DIM)), par((1, CONV_DIM)), par((1, LANE)), par((1, LANE)),
                  par((1, BW)), par((1, BW))],
        out_specs=[pl.BlockSpec((C, BW), lambda b, n: (b * nch + n, 0)),
                   pl.BlockSpec((None, CONV_W - 1, CONV_DIM), lambda b, n: (b, 0, 0)),
                   pl.BlockSpec((None, H_B, N_B, P_B), lambda b, n: (b, 0, 0, 0))],
        out_shape=[jax.ShapeDtypeStruct((B * L, BW), out_dtype),
                   jax.ShapeDtypeStruct((B, CONV_W - 1, CONV_DIM), f32),
                   jax.ShapeDtypeStruct((B, H_B, N_B, P_B), f32)],
        scratch_shapes=[pltpu.VMEM((C + 8, CONV_DIM), f32), pltpu.VMEM((H_B, N_B, P_B), f32),
                        pltpu.VMEM((C, BW), f32)],
        compiler_params=_cp("parallel", "arbitrary"),
        name="ssd",
    )(proj, proj, proj, proj, c0, s0, cw, cb, dtb, al, dsk, ng)


def _gla_kernel(q_ref, k_ref, v_ref, r_ref, sm_ref, s0_ref, w2_ref, b2_ref, ng_ref, y_ref, sf_ref,
                S_ref, *, C, c, nch):
    n = pl.program_id(1)
    nsub = C // c
    HK = H_D * DK_D

    @pl.when(n == 0)
    def _():
        S_ref[...] = s0_ref[...]

    x = jnp.dot(sm_ref[...].astype(bf16), w2_ref[...], preferred_element_type=f32) + b2_ref[...]
    la = _log_sigmoid(x) / GLA_TAU
    cum = _cumsum_rows(la)
    cum_last = cum[C - 1:C, :]
    qs = q_ref[...] * (DK_D ** -0.5)
    k = k_ref[...]
    v = v_ref[...]
    r = r_ref[...]
    ng = ng_ref[...]
    q_in = qs * jnp.exp(cum)
    k_out = k * jnp.exp(cum_last - cum)
    dec_col = jnp.broadcast_to(jnp.exp(cum_last), (8, HK)).T[:, 0:1]

    row = lax.broadcasted_iota(jnp.int32, (C, LANE), 0)
    lane = lax.broadcasted_iota(jnp.int32, (C, LANE), 1)
    causal = (lax.broadcasted_iota(jnp.int32, (C, C), 0)
              >= lax.broadcasted_iota(jnp.int32, (C, C), 1))
    for p in range(HK // LANE):
        sl = slice(p * LANE, (p + 1) * LANE)
        cum_p = cum[:, sl]
        k_p = k[:, sl]
        ref_row = jnp.zeros((C, LANE), f32)
        k_pieces = []
        for i in range(nsub):
            ref_i = cum_p[i * c - 1:i * c, :] if i > 0 else jnp.zeros((1, LANE), f32)
            in_blk = (row >= i * c) & (row < (i + 1) * c)
            ref_row = jnp.where(in_blk, ref_i, ref_row)
            expo = jnp.where(row < (i + 1) * c, ref_i - cum_p, -jnp.inf)
            k_pieces.append((k_p * jnp.exp(expo)).astype(bf16))
        k_big = jnp.concatenate(k_pieces, axis=1) if nsub > 1 else k_pieces[0]
        q_sc = qs[:, sl] * jnp.exp(cum_p - ref_row)
        for hh in range(LANE // DK_D):
            h = p * (LANE // DK_D) + hh
            in_head = (lane >= hh * DK_D) & (lane < (hh + 1) * DK_D)
            q_pieces = []
            for i in range(nsub):
                in_blk = (row >= i * c) & (row < (i + 1) * c)
                q_pieces.append(jnp.where(in_blk & in_head, q_sc, 0.0).astype(bf16))
            q_big = jnp.concatenate(q_pieces, axis=1) if nsub > 1 else q_pieces[0]
            scores = jnp.where(causal, _dot_nt(q_big, k_big), 0.0)
            vh = v[:, h * DV_D:(h + 1) * DV_D]
            s_prev = S_ref[h]
            hs = slice(h * DK_D, (h + 1) * DK_D)
            o = _dot(scores, vh) + _dot(q_in[:, hs], s_prev)
            S_ref[h] = dec_col[hs, :] * s_prev + _dot_tn(k_out[:, hs], vh)
            o = o * lax.rsqrt(jnp.mean(o * o, axis=-1, keepdims=True) + EPS)
            o = o * ng[:, h * DV_D:(h + 1) * DV_D]
            y_ref[:, h * DV_D:(h + 1) * DV_D] = (_silu(r[:, h * DV_D:(h + 1) * DV_D]) * o).astype(y_ref.dtype)

    @pl.when(n == nch - 1)
    def _():
        sf_ref[...] = S_ref[...]


def _gla(proj, s0, ls, w2, b2, ng, l, *, B, L, row_off, out_dtype):
    C = math.gcd(L, CHUNK)
    c = math.gcd(L, GLA_SUB)
    nch = L // C
    r0 = row_off // C
    row = lambda b, n: r0 + b * nch + n
    par = lambda shape: pl.BlockSpec((None,) + shape, lambda b, n: (l,) + (0,) * len(shape))
    return pl.pallas_call(
        functools.partial(_gla_kernel, C=C, c=c, nch=nch),
        grid=(B, nch),
        in_specs=[pl.BlockSpec((C, 256), lambda b, n: (row(b, n), C_DQ)),
                  pl.BlockSpec((C, 256), lambda b, n: (row(b, n), C_DK)),
                  pl.BlockSpec((C, 512), lambda b, n: (row(b, n), C_DV)),
                  pl.BlockSpec((C, 512), lambda b, n: (row(b, n), C_DR)),
                  pl.BlockSpec((C, 128), lambda b, n: (row(b, n), C_SMALL)),
                  pl.BlockSpec((None, None, H_D, DK_D, DV_D), lambda b, n: (ls, b, 0, 0, 0)),
                  par((LANE, H_D * DK_D)), par((1, H_D * DK_D)), par((1, BW))],
        out_specs=[pl.BlockSpec((C, BW), lambda b, n: (b * nch + n, 0)),
                   pl.BlockSpec((None, H_D, DK_D, DV_D), lambda b, n: (b, 0, 0, 0))],
        out_shape=[jax.ShapeDtypeStruct((B * L, BW), out_dtype),
                   jax.ShapeDtypeStruct((B, H_D, DK_D, DV_D), f32)],
        scratch_shapes=[pltpu.VMEM((H_D, DK_D, DV_D), f32)],
        compiler_params=_cp("parallel", "arbitrary"),
        name="gla",
    )(proj, proj, proj, proj, proj, s0, w2, b2, ng)


def _fox_prep_kernel(sm_ref, bf_ref, lf_ref, c_ref, ct_ref, carry_ref, *, C):
    n = pl.program_id(1)

    @pl.when(n == 0)
    def _():
        carry_ref[...] = jnp.zeros_like(carry_ref)

    lf = _log_sigmoid(sm_ref[...] + bf_ref[...])
    lf = pltpu.roll(lf, LANE - H_C, axis=1)
    lane = lax.broadcasted_iota(jnp.int32, (C, LANE), 1)
    lf = jnp.where(lane < H_C, lf, 0.0)
    cs = carry_ref[...] + _cumsum_rows(lf)
    carry_ref[...] = cs[C - 1:C, :]
    lf_ref[...] = lf[:, 0:H_C]
    c_ref[...] = cs
    ct_ref[...] = cs.T[0:H_C, :]


def _fox_prep(proj, bf, l, *, B, L, row_off):
    C = math.gcd(L, CHUNK)
    nch = L // C
    r0 = row_off // C
    return pl.pallas_call(
        functools.partial(_fox_prep_kernel, C=C),
        grid=(B, nch),
        in_specs=[pl.BlockSpec((C, LANE), lambda b, n: (r0 + b * nch + n, C_SMALL)),
                  pl.BlockSpec((None, 1, LANE), lambda b, n: (l, 0, 0))],
        out_specs=[pl.BlockSpec((C, H_C), lambda b, n: (b * nch + n, 0)),
                   pl.BlockSpec((C, LANE), lambda b, n: (b * nch + n, 0)),
                   pl.BlockSpec((None, H_C, C), lambda b, n: (b, 0, n))],
        out_shape=[jax.ShapeDtypeStruct((B * L, H_C), f32),
                   jax.ShapeDtypeStruct((B * L, LANE), f32),
                   jax.ShapeDtypeStruct((B, H_C, L), f32)],
        scratch_shapes=[pltpu.VMEM((1, LANE), f32)],
        compiler_params=_cp("parallel", "arbitrary"),
        name="fox_prep",
    )(proj, bf)


def _fox_prompt_kernel(q_ref, k_ref, v_ref, c_ref, ct_ref, y_ref, kb_ref, vt_ref, qt_ref, m_ref, l_ref,
                       acc_ref, *, tq):
    i = pl.program_id(1)

    @pl.when(i == 0)
    def _():
        kb_ref[...] = k_ref[...].astype(bf16)
        vt_ref[...] = v_ref[...].T.astype(bf16)

    qt_ref[...] = (q_ref[...] * (DH_C ** -0.5)).T.astype(bf16)
    m_ref[...] = jnp.full(m_ref.shape, -jnp.inf, f32)
    l_ref[...] = jnp.zeros(l_ref.shape, f32)
    acc_ref[...] = jnp.zeros(acc_ref.shape, f32)
    q0 = pl.multiple_of(i * tq, tq)
    key_le_query = (lax.broadcasted_iota(jnp.int32, (tq, tq), 0)
                    <= lax.broadcasted_iota(jnp.int32, (tq, tq), 1))

    def block(j, masked):
        start = pl.multiple_of(j * tq, tq)
        for h in range(H_C):
            hs = slice(h * DH_C, (h + 1) * DH_C)
            s = jnp.dot(kb_ref[pl.ds(start, tq), hs], qt_ref[hs, :], preferred_element_type=f32)
            s = s + (ct_ref[h:h + 1, pl.ds(q0, tq)] - c_ref[pl.ds(start, tq), h:h + 1])
            if masked:
                s = jnp.where(key_le_query, s, -jnp.inf)
            m_old = m_ref[h]
            m_new = jnp.maximum(m_old, jnp.max(s, axis=0, keepdims=True))
            p = jnp.exp(s - m_new)
            alpha = jnp.exp(m_old - m_new)
            l_ref[h] = alpha * l_ref[h] + jnp.sum(p, axis=0, keepdims=True)
            acc_ref[hs, :] = alpha * acc_ref[hs, :] + jnp.dot(
                vt_ref[hs, pl.ds(start, tq)], p.astype(bf16), preferred_element_type=f32)
            m_ref[h] = m_new

    def full_block(j, carry):
        block(j, False)
        return carry

    lax.fori_loop(0, i, full_block, 0)
    block(i, True)
    for h in range(H_C):
        hs = slice(h * DH_C, (h + 1) * DH_C)
        acc_ref[hs, :] = acc_ref[hs, :] / l_ref[h]
    y_ref[...] = acc_ref[...].T.astype(y_ref.dtype)


def _fox_prompt(proj, c, ct, *, B, L, out_dtype):
    tq = _pick(L, (256, 128, 64, 32, 16, 8))
    nq = L // tq
    return pl.pallas_call(
        functools.partial(_fox_prompt_kernel, tq=tq),
        grid=(B, nq),
        in_specs=[pl.BlockSpec((tq, BW), lambda b, i: (b * nq + i, C_CQ)),
                  pl.BlockSpec((L, BW), lambda b, i: (b, C_CK)),
                  pl.BlockSpec((L, BW), lambda b, i: (b, C_CV)),
                  pl.BlockSpec((L, LANE), lambda b, i: (b, 0)),
                  pl.BlockSpec((None, H_C, L), lambda b, i: (b, 0, 0))],
        out_specs=pl.BlockSpec((tq, BW), lambda b, i: (b * nq + i, 0)),
        out_shape=jax.ShapeDtypeStruct((B * L, BW), out_dtype),
        scratch_shapes=[pltpu.VMEM((L, BW), bf16), pltpu.VMEM((BW, L), bf16), pltpu.VMEM((BW, tq), bf16),
                        pltpu.VMEM((H_C, 1, tq), f32), pltpu.VMEM((H_C, 1, tq), f32),
                        pltpu.VMEM((BW, tq), f32)],
        compiler_params=_cp("parallel", "arbitrary"),
        name="fox_prompt",
    )(proj, proj, proj, c, ct)


def _fox_sample_kernel(pt_ref, q_ref, kn_ref, vn_ref, cn_ref, *refs, T, npages):
    kp = refs[0:npages]
    vp = refs[npages:2 * npages]
    lfp = refs[2 * npages:3 * npages]
    y_ref = refs[3 * npages]
    P = npages * PAGE_SIZE
    lf_t = jnp.concatenate([r[...] for r in lfp], axis=1)
    lane = lax.broadcasted_iota(jnp.int32, (H_C, P), 1)
    pre = lf_t
    k = 1
    while k < P:
        pre = pre + jnp.where(lane >= k, pltpu.roll(pre, k, axis=1), 0.0)
        k *= 2
    suffix = pre[:, P - 1:P] - pre
    q = q_ref[...] * (DH_C ** -0.5)
    kn = kn_ref[...]
    vn = vn_ref[...]
    cn_t = cn_ref[...].T[0:H_C, :]
    causal_new = (lax.broadcasted_iota(jnp.int32, (T, T), 1)
                  <= lax.broadcasted_iota(jnp.int32, (T, T), 0))
    for h in range(H_C):
        hs = slice(h * DH_C, (h + 1) * DH_C)
        qh = q[:, hs].astype(bf16)
        k_t = jnp.concatenate([r[h].astype(bf16) for r in kp], axis=1)
        v_t = jnp.concatenate([r[h].astype(bf16) for r in vp], axis=1)
        s = _dot(qh, k_t) + suffix[h:h + 1, :]
        s_new = jnp.where(causal_new, _dot_nt(qh, kn[:, hs]) - cn_t[h:h + 1, :], -jnp.inf)
        m = jnp.maximum(jnp.max(s, axis=-1, keepdims=True), jnp.max(s_new, axis=-1, keepdims=True))
        p = jnp.exp(s - m)
        p_new = jnp.exp(s_new - m)
        denom = jnp.sum(p, axis=-1, keepdims=True) + jnp.sum(p_new, axis=-1, keepdims=True)
        y_ref[:, hs] = ((_dot_nt(p, v_t) + _dot(p_new, vn[:, hs])) / denom).astype(y_ref.dtype)


def _fox_sample(proj, cnew, cache_kt, cache_vt, cache_lft, page_table, l, *, B, T, row_off, out_dtype):
    npages = page_table.shape[1]
    r0 = row_off // T
    kv_specs = [pl.BlockSpec((None, None, H_C, DH_C, PAGE_SIZE), lambda b, pt, p=p: (l, pt[b, p], 0, 0, 0))
                for p in range(npages)]
    lf_specs = [pl.BlockSpec((None, None, H_C, PAGE_SIZE), lambda b, pt, p=p: (l, pt[b, p], 0, 0))
                for p in range(npages)]
    grid_spec = pltpu.PrefetchScalarGridSpec(
        num_scalar_prefetch=1,
        grid=(B,),
        in_specs=[pl.BlockSpec((T, BW), lambda b, pt: (r0 + b, C_CQ)),
                  pl.BlockSpec((T, BW), lambda b, pt: (r0 + b, C_CK)),
                  pl.BlockSpec((T, BW), lambda b, pt: (r0 + b, C_CV)),
                  pl.BlockSpec((T, LANE), lambda b, pt: (b, 0))] + kv_specs + kv_specs + lf_specs,
        out_specs=pl.BlockSpec((T, BW), lambda b, pt: (b, 0)),
    )
    return pl.pallas_call(
        functools.partial(_fox_sample_kernel, T=T, npages=npages),
        grid_spec=grid_spec,
        out_shape=jax.ShapeDtypeStruct((B * T, BW), out_dtype),
        compiler_params=_cp("parallel"),
        name="fox_sample",
    )(page_table, proj, proj, proj, cnew, *([cache_kt] * npages), *([cache_vt] * npages),
      *([cache_lft] * npages))


def _xattn_kernel(q_ref, k_ref, v_ref, o_ref):
    q = q_ref[...]
    dh = q.shape[-1] // H_X
    for h in range(H_X):
        hs = slice(h * dh, (h + 1) * dh)
        s = _dot_nt(q[:, hs], k_ref[:, hs]) * (dh ** -0.5)
        m = jnp.max(s, axis=-1, keepdims=True)
        p = jnp.exp(s - m)
        p = p / jnp.sum(p, axis=-1, keepdims=True)
        o_ref[:, hs] = _dot(p, v_ref[:, hs]).astype(o_ref.dtype)


def _xattn(xq, mem_k, mem_v, lk, lv, *, B, L, row_off, out_dtype):
    D = xq.shape[1]
    n_mem = mem_k.shape[2]
    tq = _pick(L, (512, 256, 128, 64, 32, 16, 8))
    nq = L // tq
    r0 = row_off // tq
    return pl.pallas_call(
        _xattn_kernel,
        grid=(B, nq),
        in_specs=[pl.BlockSpec((tq, D), lambda b, i: (r0 + b * nq + i, 0)),
                  pl.BlockSpec((None, None, n_mem, D), lambda b, i: (lk, b, 0, 0)),
                  pl.BlockSpec((None, None, n_mem, D), lambda b, i: (lv, b, 0, 0))],
        out_specs=pl.BlockSpec((tq, D), lambda b, i: (b * nq + i, 0)),
        out_shape=jax.ShapeDtypeStruct((B * L, D), out_dtype),
        compiler_params=_cp("parallel", "arbitrary"),
        name="xattn_core",
    )(xq, mem_k, mem_v)


def _rope_tables(pos):
    half = DK_A // 2
    inv_freq = 1.0 / (ROPE_BASE ** (jnp.arange(half, dtype=f32) / half))
    ang = pos.astype(f32)[:, None] * inv_freq[None, :]
    cos, sin = jnp.cos(ang), jnp.sin(ang)
    return (jnp.tile(jnp.concatenate([cos, cos], axis=1), (1, H_A)),
            jnp.tile(jnp.concatenate([-sin, sin], axis=1), (1, H_A)))


def _permute_w_in(w_in):
    sizes = (256, 256, 512, 512, 512, CONV_DIM, H_B, 512, 512, 512, H_C, 256, 256, 512, 512, GLA_RANK)
    offs = [0]
    for s in sizes:
        offs.append(offs[-1] + s)
    (a_q, a_k, a_v, a_g, b_z, b_xbc, b_dt, c_q, c_k, c_v, c_f, d_q, d_k, d_v, d_r, d_lr) = (
        w_in[:, :, offs[i]:offs[i + 1]] for i in range(len(sizes)))
    d = w_in.shape[:2]
    used = 9 * 512 + 5 * 256 + H_B + H_C + GLA_RANK
    cols = [a_v, a_g, b_z, b_xbc[:, :, :BW], c_q, c_k, c_v, d_v, d_r,
            a_q, a_k, b_xbc[:, :, BW:], d_q, d_k,
            b_dt, c_f, d_lr, jnp.zeros(d + (NP_COLS - used,), w_in.dtype)]
    w_mix = jnp.concatenate(cols, axis=2).astype(bf16)
    w_gate = w_in[:, :, offs[-1]:].astype(bf16)
    return w_mix, w_gate


def _pad_lanes(x, start):
    depth, n = x.shape
    return jnp.zeros((depth, 1, LANE), f32).at[:, 0, start:start + n].set(x)


def kernel(x_prompt, x_sample, cache_fox_k, cache_fox_v, cache_fox_logf, cache_mem_k, cache_mem_v, state_ret, state_ssm, state_conv, state_gla, page_table, mem_prompt, w_in, b_gate, ret_norm_g, ssd_conv_w, ssd_conv_b, ssd_dt_bias, ssd_a_log, ssd_d, ssd_norm_g, fox_bf, gla_w_lr2, gla_b_lr, gla_norm_g, w_branch, w_out, norm_mix_g, norm_x_g, norm_mem_g, w_xq, w_mem_kv, w_xo, norm_ffn_g, w_ffn_up, w_ffn_down, norm_final_g):
    Bp, Lp, D = x_prompt.shape
    Bs, Ls, _ = x_sample.shape
    depth = w_in.shape[0]
    n_mem = mem_prompt.shape[1]
    past_len = page_table.shape[1] * PAGE_SIZE
    Tp, Ts = Bp * Lp, Bs * Ls

    w_mix, w_gate = _permute_w_in(w_in)
    w_branch_b = w_branch.astype(bf16)
    w_out_b = w_out.astype(bf16)
    w_xq_b = w_xq.astype(bf16)
    w_xo_b = w_xo.astype(bf16)
    w_kv_b = w_mem_kv.astype(bf16)
    w_up_b = w_ffn_up.astype(bf16)
    w_down_b = w_ffn_down.astype(bf16)
    b_gate4 = b_gate.reshape(depth, N_BRANCH, 1, D)
    g_mix = norm_mix_g.reshape(depth, 1, D)
    g_x = norm_x_g.reshape(depth, 1, D)
    g_mem = norm_mem_g.reshape(depth, 1, D)
    g_ffn = norm_ffn_g.reshape(depth, 1, D)
    ret_g = ret_norm_g.reshape(depth, 1, BW)
    gla_g = gla_norm_g.reshape(depth, 1, BW)
    ssd_g = ssd_norm_g.reshape(depth, 1, BW)
    conv_b = ssd_conv_b.reshape(depth, 1, CONV_DIM)
    dt_bias = _pad_lanes(ssd_dt_bias, 0)
    a_log = _pad_lanes(ssd_a_log, 0)
    d_skip = jnp.repeat(ssd_d, P_B, axis=1).reshape(depth, 1, BW)
    fox_b = _pad_lanes(fox_bf, H_B)
    w_lr2 = jnp.zeros((depth, LANE, H_D * DK_D), f32).at[:, H_B + H_C:H_B + H_C + GLA_RANK, :].set(
        gla_w_lr2).astype(bf16)
    b_lr = gla_b_lr.reshape(depth, 1, H_D * DK_D)
    cos_p, sin_p = _rope_tables(jnp.arange(Lp, dtype=jnp.int32))
    cos_s, sin_s = _rope_tables(past_len + jnp.arange(Ls, dtype=jnp.int32))
    ck_t = jnp.transpose(cache_fox_k, (0, 1, 3, 4, 2))
    cv_t = jnp.transpose(cache_fox_v, (0, 1, 3, 4, 2))
    clf_t = jnp.transpose(cache_fox_logf, (0, 1, 3, 2))
    cmk = cache_mem_k.reshape(cache_mem_k.shape[:3] + (D,))
    cmv = cache_mem_v.reshape(cache_mem_v.shape[:3] + (D,))
    z_ret = jnp.zeros((1, Bp, H_A, DK_A, DV_A), f32)
    z_ssm = jnp.zeros((1, Bp, H_B, N_B, P_B), f32)
    z_conv = jnp.zeros((1, Bp, CONV_W - 1, CONV_DIM), f32)
    z_gla = jnp.zeros((1, Bp, H_D, DK_D, DV_D), f32)
    mem2d = mem_prompt.reshape(Bp * n_mem, D)

    h = jnp.concatenate([x_prompt.reshape(Tp, D), x_sample.reshape(Ts, D)], axis=0)
    outs = {k: [] for k in ("ret_p", "ssm_p", "conv_p", "gla_p", "fk_p", "fv_p", "flf_p", "mk_p", "mv_p",
                            "ret_s", "ssm_s", "conv_s", "gla_s", "fk_s", "fv_s", "flf_s")}
    for l in range(depth):
        proj = _norm_matmul(h, g_mix, w_mix, l, tn=1024)

        pk = dict(B=Bp, L=Lp, row_off=0, out_dtype=bf16)
        sk = dict(B=Bs, L=Ls, row_off=Tp, out_dtype=f32)
        ya_p, ret_p = _retention(proj, cos_p, sin_p, z_ret, 0, ret_g, l, **pk)
        ya_s, ret_s = _retention(proj, cos_s, sin_s, state_ret, l, ret_g, l, **sk)
        yb_p, conv_p, ssm_p = _ssd(proj, z_conv, z_ssm, 0, ssd_conv_w, conv_b, dt_bias, a_log, d_skip,
                                   ssd_g, l, **pk)
        yb_s, conv_s, ssm_s = _ssd(proj, state_conv, state_ssm, l, ssd_conv_w, conv_b, dt_bias, a_log,
                                   d_skip, ssd_g, l, **sk)
        lf_p, c_p, ct_p = _fox_prep(proj, fox_b, l, B=Bp, L=Lp, row_off=0)
        lf_s, c_s, _ = _fox_prep(proj, fox_b, l, B=Bs, L=Ls, row_off=Tp)
        yc_p = _fox_prompt(proj, c_p, ct_p, B=Bp, L=Lp, out_dtype=bf16)
        yc_s = _fox_sample(proj, c_s, ck_t, cv_t, clf_t, page_table, l, B=Bs, T=Ls, row_off=Tp, out_dtype=f32)
        yd_p, gla_p = _gla(proj, z_gla, 0, w_lr2, b_lr, gla_g, l, **pk)
        yd_s, gla_s = _gla(proj, state_gla, l, w_lr2, b_lr, gla_g, l, **sk)

        h = _merge(h, g_mix, (ya_p, yb_p, yc_p, yd_p), (ya_s, yb_s, yc_s, yd_s),
                   w_gate, b_gate4, w_branch_b, w_out_b, l)

        kv = _norm_matmul(mem2d, g_mem, w_kv_b, l, tn=D, split=True)
        mem_kv = kv.reshape(2, Bp, n_mem, D)
        xq = _norm_matmul(h, g_x, w_xq_b, l, tn=D)
        o_p = _xattn(xq, mem_kv, mem_kv, 0, 1, B=Bp, L=Lp, row_off=0, out_dtype=bf16)
        o_s = _xattn(xq, cmk, cmv, l, l, B=Bs, L=Ls, row_off=Tp, out_dtype=f32)
        h = _xo(h, o_p, o_s, w_xo_b, l)
        h = _ffn(h, g_ffn, w_up_b, w_down_b, l)

        ckv = proj[:, C_CK * BW:(C_CV + 1) * BW]
        outs["ret_p"].append(ret_p); outs["ssm_p"].append(ssm_p); outs["conv_p"].append(conv_p)
        outs["gla_p"].append(gla_p)
        outs["fk_p"].append(ckv[:Tp, :BW].reshape(Bp, Lp, H_C, DH_C))
        outs["fv_p"].append(ckv[:Tp, BW:].reshape(Bp, Lp, H_C, DH_C))
        outs["flf_p"].append(lf_p.reshape(Bp, Lp, H_C))
        outs["mk_p"].append(kv[0].reshape(Bp, n_mem, H_X, D // H_X))
        outs["mv_p"].append(kv[1].reshape(Bp, n_mem, H_X, D // H_X))
        outs["ret_s"].append(ret_s); outs["ssm_s"].append(ssm_s); outs["conv_s"].append(conv_s)
        outs["gla_s"].append(gla_s)
        outs["fk_s"].append(ckv[Tp:, :BW].reshape(Bs, Ls, H_C, DH_C))
        outs["fv_s"].append(ckv[Tp:, BW:].reshape(Bs, Ls, H_C, DH_C))
        outs["flf_s"].append(lf_s.reshape(Bs, Ls, H_C))

    y = _final_norm(h, norm_final_g.reshape(1, D))
    st = {k: jnp.stack(v, axis=0) for k, v in outs.items()}
    return (y[:Tp].reshape(Bp, Lp, D), y[Tp:].reshape(Bs, Ls, D),
            st["ret_p"], st["ssm_p"], st["conv_p"], st["gla_p"], st["fk_p"], st["fv_p"], st["flf_p"],
            st["mk_p"], st["mv_p"],
            st["ret_s"], st["ssm_s"], st["conv_s"], st["gla_s"], st["fk_s"], st["fv_s"], st["flf_s"])
```

```python
import functools
import math

import jax
import jax.numpy as jnp
from jax import lax
from jax.experimental import pallas as pl
from jax.experimental.pallas import tpu as pltpu

f32 = jnp.float32
bf16 = jnp.bfloat16

EPS = 1e-6
CHUNK = 128
GLA_CHUNK = 16
GLA_SUB = 16
CHUNKS_PER_STEP = 2
FOX_TQ, FOX_TK = 512, 512
ROPE_BASE = 10000.0
PAGE_SIZE = 128
H_A, DK_A, DV_A = 4, 64, 128
H_B, P_B, N_B, G_B, CONV_W = 8, 64, 64, 2, 4
H_C, DH_C = 8, 64
H_D, DK_D, DV_D = 4, 64, 128
GLA_RANK = 16
GLA_TAU = 16.0
N_BRANCH = 4
H_X = 4
BW = 512
CONV_DIM = BW + 2 * G_B * N_B

C_AV, C_AG, C_BZ, C_BXS, C_CQ, C_CK, C_CV, C_DV, C_DR = range(9)
C_AQ, C_AK, C_BBC, C_DQ, C_DK = range(18, 23)
C_SMALL = 46
NP_COLS = 6144
LANE = 128
VMEM_LIMIT = 56 * 1024 * 1024


def _cp(*sem):
    return pltpu.CompilerParams(dimension_semantics=sem, vmem_limit_bytes=VMEM_LIMIT)


def _pick(n, cands):
    for c in cands:
        if n % c == 0:
            return c
    raise ValueError(f"no tile for {n}")


def _dot(a, b):
    return jnp.dot(a.astype(bf16), b.astype(bf16), preferred_element_type=f32)


def _dot_nt(a, b):
    return lax.dot_general(a.astype(bf16), b.astype(bf16), (((1,), (1,)), ((), ())),
                           preferred_element_type=f32)


def _dot_tn(a, b):
    return lax.dot_general(a.astype(bf16), b.astype(bf16), (((0,), (0,)), ((), ())),
                           preferred_element_type=f32)


def _silu(x):
    return x * jax.nn.sigmoid(x)


def _log_sigmoid(x):
    return jnp.minimum(x, 0.0) - jnp.log1p(jnp.exp(-jnp.abs(x)))


def _softplus(x):
    return jnp.maximum(x, 0.0) + jnp.log1p(jnp.exp(-jnp.abs(x)))


def _cumsum_rows(x, seg=None):
    n = x.shape[0] if seg is None else seg
    row = lax.broadcasted_iota(jnp.int32, x.shape, 0)
    if n != x.shape[0]:
        row = row % n
    k = 1
    while k < n:
        x = x + jnp.where(row >= k, pltpu.roll(x, k, axis=0), 0.0)
        k *= 2
    return x


def _rms(x, g):
    return x * lax.rsqrt(jnp.mean(x * x, axis=-1, keepdims=True) + EPS) * g


def _norm_matmul_kernel(x_ref, g_ref, w_ref, o_ref, xn_ref):
    @pl.when(pl.program_id(1) == 0)
    def _():
        xn_ref[...] = _rms(x_ref[...], g_ref[...]).astype(bf16)

    o_ref[...] = jnp.dot(xn_ref[...], w_ref[...], preferred_element_type=f32).astype(o_ref.dtype)


def _norm_matmul(x, g, w, l, *, tn, out_dtype=f32, split=False):
    M, K = x.shape
    N = w.shape[2]
    tm = _pick(M, (1024, 512, 256, 128, 64, 32, 16, 8))
    if split:
        out_shape = jax.ShapeDtypeStruct((N // tn, M, tn), out_dtype)
        out_spec = pl.BlockSpec((None, tm, tn), lambda i, j: (j, i, 0))
    else:
        out_shape = jax.ShapeDtypeStruct((M, N), out_dtype)
        out_spec = pl.BlockSpec((tm, tn), lambda i, j: (i, j))
    return pl.pallas_call(
        _norm_matmul_kernel,
        grid=(M // tm, N // tn),
        in_specs=[pl.BlockSpec((tm, K), lambda i, j: (i, 0)),
                  pl.BlockSpec((None, 1, K), lambda i, j: (l, 0, 0)),
                  pl.BlockSpec((None, K, tn), lambda i, j: (l, 0, j))],
        out_specs=out_spec,
        out_shape=out_shape,
        scratch_shapes=[pltpu.VMEM((tm, K), bf16)],
        compiler_params=_cp("parallel", "arbitrary"),
        name="norm_matmul",
    )(x, g, w)


def _two_src(tm, width, n_p, nargs):
    if nargs == 1:
        return (pl.BlockSpec((tm, width), lambda m: (jnp.minimum(m, n_p - 1), 0)),
                pl.BlockSpec((tm, width), lambda m: (jnp.maximum(m - n_p, 0), 0)))
    return (pl.BlockSpec((tm, width), lambda m, i: (jnp.minimum(m, n_p - 1), 0)),
            pl.BlockSpec((tm, width), lambda m, i: (jnp.maximum(m - n_p, 0), 0)))


def _merge_kernel(h_ref, g_ref, *refs, n_p):
    yp = refs[0:4]
    ys = refs[4:8]
    wg_ref, bg_ref, wb_ref, wo_ref, o_ref, xn_ref, acc_ref = refs[8:]
    m = pl.program_id(0)
    i = pl.program_id(1)

    @pl.when(i == 0)
    def _():
        xn_ref[...] = _rms(h_ref[...], g_ref[...]).astype(bf16)
        acc_ref[...] = jnp.zeros_like(acc_ref)

    gate = jax.nn.sigmoid(jnp.dot(xn_ref[...], wg_ref[...], preferred_element_type=f32) + bg_ref[...])
    for k in range(N_BRANCH):
        @pl.when(i == k)
        def _(k=k):
            y = jnp.where(m < n_p, yp[k][...].astype(bf16), ys[k][...].astype(bf16))
            acc_ref[...] += gate * jnp.dot(y, wb_ref[...], preferred_element_type=f32)

    @pl.when(i == N_BRANCH - 1)
    def _():
        o_ref[...] = h_ref[...] + jnp.dot(acc_ref[...].astype(bf16), wo_ref[...],
                                          preferred_element_type=f32)


def _merge(h, g, yps, yss, w_gate, b_gate, w_branch, w_out, l):
    T, D = h.shape
    Tp, Ts = yps[0].shape[0], yss[0].shape[0]
    tm = _pick(math.gcd(Tp, Ts), (512, 256, 128, 64, 32, 16, 8))
    n_p = Tp // tm
    sp, ss = _two_src(tm, BW, n_p, 2)
    return pl.pallas_call(
        functools.partial(_merge_kernel, n_p=n_p),
        grid=(T // tm, N_BRANCH),
        in_specs=[pl.BlockSpec((tm, D), lambda m, i: (m, 0)),
                  pl.BlockSpec((None, 1, D), lambda m, i: (l, 0, 0))]
                 + [sp] * 4 + [ss] * 4
                 + [pl.BlockSpec((None, D, D), lambda m, i: (l, 0, i)),
                    pl.BlockSpec((None, None, 1, D), lambda m, i: (l, i, 0, 0)),
                    pl.BlockSpec((None, None, BW, D), lambda m, i: (l, i, 0, 0)),
                    pl.BlockSpec((None, D, D), lambda m, i: (l, 0, 0))],
        out_specs=pl.BlockSpec((tm, D), lambda m, i: (m, 0)),
        out_shape=jax.ShapeDtypeStruct((T, D), f32),
        scratch_shapes=[pltpu.VMEM((tm, D), bf16), pltpu.VMEM((tm, D), f32)],
        compiler_params=_cp("parallel", "arbitrary"),
        name="branch_merge",
    )(h, g, *yps, *yss, w_gate, b_gate, w_branch, w_out)


def _xo_kernel(h_ref, op_ref, os_ref, w_ref, o_ref, *, n_p):
    m = pl.program_id(0)
    o = jnp.where(m < n_p, op_ref[...].astype(bf16), os_ref[...].astype(bf16))
    o_ref[...] = h_ref[...] + jnp.dot(o, w_ref[...], preferred_element_type=f32)


def _xo(h, o_p, o_s, w, l):
    T, D = h.shape
    Tp, Ts = o_p.shape[0], o_s.shape[0]
    tm = _pick(math.gcd(Tp, Ts), (1024, 512, 256, 128, 64, 32, 16, 8))
    n_p = Tp // tm
    sp, ss = _two_src(tm, D, n_p, 1)
    return pl.pallas_call(
        functools.partial(_xo_kernel, n_p=n_p),
        grid=(T // tm,),
        in_specs=[pl.BlockSpec((tm, D), lambda m: (m, 0)), sp, ss,
                  pl.BlockSpec((None, D, D), lambda m: (l, 0, 0))],
        out_specs=pl.BlockSpec((tm, D), lambda m: (m, 0)),
        out_shape=jax.ShapeDtypeStruct((T, D), f32),
        compiler_params=_cp("parallel"),
        name="xattn_out",
    )(h, o_p, o_s, w)


def _ffn_kernel(h_ref, g_ref, wa_ref, wb_ref, wd_ref, o_ref, xn_ref, acc_ref, *, nj):
    j = pl.program_id(1)

    @pl.when(j == 0)
    def _():
        xn_ref[...] = _rms(h_ref[...], g_ref[...]).astype(bf16)
        acc_ref[...] = jnp.zeros_like(acc_ref)

    xn = xn_ref[...]
    a = jnp.dot(xn, wa_ref[...], preferred_element_type=f32)
    b = jnp.dot(xn, wb_ref[...], preferred_element_type=f32)
    acc_ref[...] += jnp.dot((_silu(a) * b).astype(bf16), wd_ref[...], preferred_element_type=f32)

    @pl.when(j == nj - 1)
    def _():
        o_ref[...] = h_ref[...] + acc_ref[...]


def _ffn(h, g, w_up, w_down, l):
    T, D = h.shape
    d_ff = w_down.shape[1]
    tm = _pick(T, (1024, 512, 256, 128, 64, 32, 16, 8))
    tf = _pick(d_ff, (256, 128))
    nj = d_ff // tf
    return pl.pallas_call(
        functools.partial(_ffn_kernel, nj=nj),
        grid=(T // tm, nj),
        in_specs=[pl.BlockSpec((tm, D), lambda m, j: (m, 0)),
                  pl.BlockSpec((None, 1, D), lambda m, j: (l, 0, 0)),
                  pl.BlockSpec((None, D, tf), lambda m, j: (l, 0, j)),
                  pl.BlockSpec((None, D, tf), lambda m, j: (l, 0, j + nj)),
                  pl.BlockSpec((None, tf, D), lambda m, j: (l, j, 0))],
        out_specs=pl.BlockSpec((tm, D), lambda m, j: (m, 0)),
        out_shape=jax.ShapeDtypeStruct((T, D), f32),
        scratch_shapes=[pltpu.VMEM((tm, D), bf16), pltpu.VMEM((tm, D), f32)],
        compiler_params=_cp("parallel", "arbitrary"),
        name="swiglu_ffn",
    )(h, g, w_up, w_up, w_down)


def _final_norm_kernel(x_ref, g_ref, o_ref):
    o_ref[...] = _rms(x_ref[...], g_ref[...])


def _final_norm(h, g):
    T, D = h.shape
    tm = _pick(T, (1024, 512, 256, 128, 64, 32, 16, 8))
    return pl.pallas_call(
        _final_norm_kernel,
        grid=(T // tm,),
        in_specs=[pl.BlockSpec((tm, D), lambda m: (m, 0)), pl.BlockSpec((1, D), lambda m: (0, 0))],
        out_specs=pl.BlockSpec((tm, D), lambda m: (m, 0)),
        out_shape=jax.ShapeDtypeStruct((T, D), f32),
        compiler_params=_cp("parallel"),
        name="final_norm",
    )(h, g)


def _units_per_step(B, nchunks):
    if nchunks == 1:
        return _pick(B, (8, 4, 2, 1)), 1
    return 1, _pick(nchunks, (CHUNKS_PER_STEP, 1))


def _ret_kernel(q_ref, k_ref, v_ref, g_ref, cos_ref, sin_ref, s0_ref, ng_ref, y_ref, sf_ref, S_ref,
                *, C, nch, SB, CB):
    n = pl.program_id(1)

    @pl.when(n == 0)
    def _():
        S_ref[...] = s0_ref[...]

    lane = lax.broadcasted_iota(jnp.int32, (C, H_A * DK_A), 1)
    first_half = (lane % DK_A) < (DK_A // 2)

    def rot(x, cos, sin):
        fwd = pltpu.roll(x, DK_A // 2, axis=1)
        bwd = pltpu.roll(x, H_A * DK_A - DK_A // 2, axis=1)
        return x * cos + jnp.where(first_half, bwd, fwd) * sin

    ng = ng_ref[...]
    t_col = lax.broadcasted_iota(jnp.int32, (C, 1), 0).astype(f32)
    diff = (lax.broadcasted_iota(jnp.int32, (C, C), 0)
            - lax.broadcasted_iota(jnp.int32, (C, C), 1))
    for u in range(SB * CB):
        s, cb = divmod(u, CB)
        rs = slice(u * C, (u + 1) * C)
        cos = cos_ref[cb * C:(cb + 1) * C, :]
        sin = sin_ref[cb * C:(cb + 1) * C, :]
        q = rot(q_ref[rs, :], cos, sin)
        k = rot(k_ref[rs, :], cos, sin) * (DK_A ** -0.5)
        v = v_ref[rs, :]
        gate = g_ref[rs, :]
        for h in range(H_A):
            lg = math.log1p(-2.0 ** (-5.0 - h))
            qh = q[:, h * DK_A:(h + 1) * DK_A]
            kh = k[:, h * DK_A:(h + 1) * DK_A]
            vs = slice(h * DV_A, (h + 1) * DV_A)
            vh = v[:, vs]
            decay = jnp.where(diff >= 0, jnp.exp(diff.astype(f32) * lg), 0.0)
            scores = _dot_nt(qh, kh) * decay
            s_prev = S_ref[s, h]
            o = _dot(scores, vh) + _dot(qh * jnp.exp((t_col + 1.0) * lg), s_prev)
            k_out = kh * jnp.exp((C - 1.0 - t_col) * lg)
            S_ref[s, h] = math.exp(C * lg) * s_prev + _dot_tn(k_out, vh)
            o = o - jnp.mean(o, axis=-1, keepdims=True)
            o = o * lax.rsqrt(jnp.mean(o * o, axis=-1, keepdims=True) + EPS)
            y_ref[rs, vs] = (_silu(gate[:, vs]) * (o * ng[:, vs])).astype(y_ref.dtype)

    @pl.when(n == nch - 1)
    def _():
        sf_ref[...] = S_ref[...]


def _retention(proj, cos, sin, s0, ls, ng, l, *, B, L, row_off, out_dtype):
    C = math.gcd(L, CHUNK)
    SB, CB = _units_per_step(B, L // C)
    nch = L // (C * CB)
    R = SB * CB * C
    r0 = row_off // R
    row = lambda b, n: r0 + b * nch + n
    return pl.pallas_call(
        functools.partial(_ret_kernel, C=C, nch=nch, SB=SB, CB=CB),
        grid=(B // SB, nch),
        in_specs=[pl.BlockSpec((R, 256), lambda b, n: (row(b, n), C_AQ)),
                  pl.BlockSpec((R, 256), lambda b, n: (row(b, n), C_AK)),
                  pl.BlockSpec((R, 512), lambda b, n: (row(b, n), C_AV)),
                  pl.BlockSpec((R, 512), lambda b, n: (row(b, n), C_AG)),
                  pl.BlockSpec((CB * C, 256), lambda b, n: (n, 0)),
                  pl.BlockSpec((CB * C, 256), lambda b, n: (n, 0)),
                  pl.BlockSpec((None, SB, H_A, DK_A, DV_A), lambda b, n: (ls, b, 0, 0, 0)),
                  pl.BlockSpec((None, 1, BW), lambda b, n: (l, 0, 0))],
        out_specs=[pl.BlockSpec((R, BW), lambda b, n: (b * nch + n, 0)),
                   pl.BlockSpec((SB, H_A, DK_A, DV_A), lambda b, n: (b, 0, 0, 0))],
        out_shape=[jax.ShapeDtypeStruct((B * L, BW), out_dtype),
                   jax.ShapeDtypeStruct((B, H_A, DK_A, DV_A), f32)],
        scratch_shapes=[pltpu.VMEM((SB, H_A, DK_A, DV_A), f32)],
        compiler_params=_cp("parallel", "arbitrary"),
        name="retention",
    )(proj, proj, proj, proj, cos, sin, s0, ng)


def _ssd_kernel(z_ref, xs_ref, bc_ref, sm_ref, c0_ref, s0_ref, cw_ref, cb_ref, dtb_ref, al_ref,
                dsk_ref, ng_ref, y_ref, cn_ref, sf_ref, xpad_ref, S_ref, o_ref, *, C, nch, SB, CB):
    n = pl.program_id(1)
    P0 = 8 - (CONV_W - 1)

    @pl.when(n == 0)
    def _():
        xpad_ref[:, P0:8, :] = c0_ref[...]
        S_ref[...] = s0_ref[...]

    cw = cw_ref[...]
    dsk = dsk_ref[...]
    causal = (lax.broadcasted_iota(jnp.int32, (C, C), 0)
              >= lax.broadcasted_iota(jnp.int32, (C, C), 1))
    rep = H_B // G_B
    for u in range(SB * CB):
        s = u // CB
        rs = slice(u * C, (u + 1) * C)
        xpad_ref[s, 8:8 + C, 0:BW] = xs_ref[rs, :]
        xpad_ref[s, 8:8 + C, BW:CONV_DIM] = bc_ref[rs, :]
        conv = cb_ref[...] + xpad_ref[s, P0:P0 + C, :] * cw[0:1, :]
        for j in range(1, CONV_W):
            conv = conv + xpad_ref[s, P0 + j:P0 + j + C, :] * cw[j:j + 1, :]
        tail = xpad_ref[s, C + P0:C + 8, :]
        cn_ref[s] = tail
        xpad_ref[s, P0:8, :] = tail

        act = _silu(conv)
        xs = act[:, 0:BW]
        dt = _softplus(sm_ref[rs, :] + dtb_ref[...])
        la = dt * (-jnp.exp(al_ref[...]))
        cum = _cumsum_rows(la)
        cum_t = cum.T
        cum_last = cum[C - 1:C, :]
        for g in range(G_B):
            bm = act[:, BW + g * N_B:BW + (g + 1) * N_B]
            cm = act[:, BW + G_B * N_B + g * N_B:BW + G_B * N_B + (g + 1) * N_B]
            scores_g = _dot_nt(cm, bm)
            for h in range(g * rep, (g + 1) * rep):
                ps = slice(h * P_B, (h + 1) * P_B)
                cum_h = cum[:, h:h + 1]
                seg = cum_h - cum_t[h:h + 1, :]
                decay = jnp.exp(jnp.where(causal, seg, -jnp.inf))
                xh = xs[:, ps]
                vh = xh * dt[:, h:h + 1]
                s_prev = S_ref[s, h]
                o = _dot(scores_g * decay, vh) + _dot(cm * jnp.exp(cum_h), s_prev)
                k_out = bm * jnp.exp(cum_last[:, h:h + 1] - cum_h)
                S_ref[s, h] = jnp.exp(cum_last[:, h:h + 1]) * s_prev + _dot_tn(k_out, vh)
                o_ref[u, :, ps] = o + dsk[:, ps] * xh
        y = o_ref[u] * _silu(z_ref[rs, :])
        y_ref[rs, :] = _rms(y, ng_ref[...]).astype(y_ref.dtype)

    @pl.when(n == nch - 1)
    def _():
        sf_ref[...] = S_ref[...]


def _ssd(proj, c0, s0, ls, cw, cb, dtb, al, dsk, ng, l, *, B, L, row_off, out_dtype):
    C = math.gcd(L, CHUNK)
    SB, CB = _units_per_step(B, L // C)
    nch = L // (C * CB)
    R = SB * CB * C
    r0 = row_off // R
    row = lambda b, n: r0 + b * nch + n
    par = lambda shape: pl.BlockSpec((None,) + shape, lambda b, n: (l,) + (0,) * len(shape))
    return pl.pallas_call(
        functools.partial(_ssd_kernel, C=C, nch=nch, SB=SB, CB=CB),
        grid=(B // SB, nch),
        in_specs=[pl.BlockSpec((R, 512), lambda b, n: (row(b, n), C_BZ)),
                  pl.BlockSpec((R, 512), lambda b, n: (row(b, n), C_BXS)),
                  pl.BlockSpec((R, 256), lambda b, n: (row(b, n), C_BBC)),
                  pl.BlockSpec((R, 128), lambda b, n: (row(b, n), C_SMALL)),
                  pl.BlockSpec((None, SB, CONV_W - 1, CONV_DIM), lambda b, n: (ls, b, 0, 0)),
                  pl.BlockSpec((None, SB, H_B, N_B, P_B), lambda b, n: (ls, b, 0, 0, 0)),
                  par((CONV_W, CONV_DIM)), par((1, CONV_DIM)), par((1, LANE)), par((1, LANE)),
                  par((1, BW)), par((1, BW))],
        out_specs=[pl.BlockSpec((R, BW), lambda b, n: (b * nch + n, 0)),
                   pl.BlockSpec((SB, CONV_W - 1, CONV_DIM), lambda b, n: (b, 0, 0)),
                   pl.BlockSpec((SB, H_B, N_B, P_B), lambda b, n: (b, 0, 0, 0))],
        out_shape=[jax.ShapeDtypeStruct((B * L, BW), out_dtype),
                   jax.ShapeDtypeStruct((B, CONV_W - 1, CONV_DIM), f32),
                   jax.ShapeDtypeStruct((B, H_B, N_B, P_B), f32)],
        scratch_shapes=[pltpu.VMEM((SB, C + 8, CONV_DIM), f32), pltpu.VMEM((SB, H_B, N_B, P_B), f32),
                        pltpu.VMEM((SB * CB, C, BW), f32)],
        compiler_params=_cp("parallel", "arbitrary"),
        name="ssd",
    )(proj, proj, proj, proj, c0, s0, cw, cb, dtb, al, dsk, ng)


def _gla_kernel(q_ref, k_ref, v_ref, r_ref, sm_ref, s0_ref, w2_ref, b2_ref, ng_ref, y_ref, sf_ref,
                S_ref, *, C, c, nch, SB, CB):
    n = pl.program_id(1)

    @pl.when(n == 0)
    def _():
        S_ref[...] = s0_ref[...]

    x_all = jnp.dot(sm_ref[...].astype(bf16), w2_ref[...], preferred_element_type=f32) + b2_ref[...]
    la_all = _log_sigmoid(x_all) / GLA_TAU
    for u in range(SB * CB):
        rs = slice(u * C, (u + 1) * C)
        _gla_one_chunk(la_all[rs, :], q_ref[rs, :], k_ref[rs, :], v_ref[rs, :], r_ref[rs, :], ng_ref[...],
                       y_ref, rs, S_ref, u // CB, C=C, c=c)

    @pl.when(n == nch - 1)
    def _():
        sf_ref[...] = S_ref[...]


def _gla_one_chunk(la, q, k, v, r, ng, y_ref, rs, S_ref, s, *, C, c):
    nsub = C // c
    HK = H_D * DK_D
    cum = _cumsum_rows(la)
    cum_last = cum[C - 1:C, :]
    qs = q * (DK_D ** -0.5)
    q_in = qs * jnp.exp(cum)
    k_out = k * jnp.exp(cum_last - cum)
    dec_col = jnp.broadcast_to(jnp.exp(cum_last), (8, HK)).T[:, 0:1]

    row = lax.broadcasted_iota(jnp.int32, (C, LANE), 0)
    lane = lax.broadcasted_iota(jnp.int32, (C, LANE), 1)
    causal = (lax.broadcasted_iota(jnp.int32, (C, C), 0)
              >= lax.broadcasted_iota(jnp.int32, (C, C), 1))
    for p in range(HK // LANE):
        sl = slice(p * LANE, (p + 1) * LANE)
        cum_p = cum[:, sl]
        k_p = k[:, sl]
        ref_row = jnp.zeros((C, LANE), f32)
        k_pieces = []
        for i in range(nsub):
            ref_i = cum_p[i * c - 1:i * c, :] if i > 0 else jnp.zeros((1, LANE), f32)
            in_blk = (row >= i * c) & (row < (i + 1) * c)
            ref_row = jnp.where(in_blk, ref_i, ref_row)
            expo = jnp.where(row < (i + 1) * c, ref_i - cum_p, -jnp.inf)
            k_pieces.append((k_p * jnp.exp(expo)).astype(bf16))
        k_big = jnp.concatenate(k_pieces, axis=1) if nsub > 1 else k_pieces[0]
        q_sc = qs[:, sl] * jnp.exp(cum_p - ref_row)
        for hh in range(LANE // DK_D):
            h = p * (LANE // DK_D) + hh
            in_head = (lane >= hh * DK_D) & (lane < (hh + 1) * DK_D)
            q_pieces = []
            for i in range(nsub):
                in_blk = (row >= i * c) & (row < (i + 1) * c)
                q_pieces.append(jnp.where(in_blk & in_head, q_sc, 0.0).astype(bf16))
            q_big = jnp.concatenate(q_pieces, axis=1) if nsub > 1 else q_pieces[0]
            scores = jnp.where(causal, _dot_nt(q_big, k_big), 0.0)
            vs = slice(h * DV_D, (h + 1) * DV_D)
            vh = v[:, vs]
            s_prev = S_ref[s, h]
            hs = slice(h * DK_D, (h + 1) * DK_D)
            o = _dot(scores, vh) + _dot(q_in[:, hs], s_prev)
            S_ref[s, h] = dec_col[hs, :] * s_prev + _dot_tn(k_out[:, hs], vh)
            o = o * lax.rsqrt(jnp.mean(o * o, axis=-1, keepdims=True) + EPS)
            y_ref[rs, vs] = (_silu(r[:, vs]) * (o * ng[:, vs])).astype(y_ref.dtype)


def _gla(proj, s0, ls, w2, b2, ng, l, *, B, L, row_off, out_dtype):
    C = math.gcd(L, CHUNK)
    c = math.gcd(L, GLA_SUB)
    SB, CB = _units_per_step(B, L // C)
    nch = L // (C * CB)
    R = SB * CB * C
    r0 = row_off // R
    row = lambda b, n: r0 + b * nch + n
    par = lambda shape: pl.BlockSpec((None,) + shape, lambda b, n: (l,) + (0,) * len(shape))
    return pl.pallas_call(
        functools.partial(_gla_kernel, C=C, c=c, nch=nch, SB=SB, CB=CB),
        grid=(B // SB, nch),
        in_specs=[pl.BlockSpec((R, 256), lambda b, n: (row(b, n), C_DQ)),
                  pl.BlockSpec((R, 256), lambda b, n: (row(b, n), C_DK)),
                  pl.BlockSpec((R, 512), lambda b, n: (row(b, n), C_DV)),
                  pl.BlockSpec((R, 512), lambda b, n: (row(b, n), C_DR)),
                  pl.BlockSpec((R, 128), lambda b, n: (row(b, n), C_SMALL)),
                  pl.BlockSpec((None, SB, H_D, DK_D, DV_D), lambda b, n: (ls, b, 0, 0, 0)),
                  par((LANE, H_D * DK_D)), par((1, H_D * DK_D)), par((1, BW))],
        out_specs=[pl.BlockSpec((R, BW), lambda b, n: (b * nch + n, 0)),
                   pl.BlockSpec((SB, H_D, DK_D, DV_D), lambda b, n: (b, 0, 0, 0))],
        out_shape=[jax.ShapeDtypeStruct((B * L, BW), out_dtype),
                   jax.ShapeDtypeStruct((B, H_D, DK_D, DV_D), f32)],
        scratch_shapes=[pltpu.VMEM((SB, H_D, DK_D, DV_D), f32)],
        compiler_params=_cp("parallel", "arbitrary"),
        name="gla",
    )(proj, proj, proj, proj, proj, s0, w2, b2, ng)


def _fox_prep_kernel(sm_ref, bf_ref, lf_ref, c_ref, *rest, C, head_major, chained):
    rest = list(rest)
    ct_ref = rest.pop(0) if head_major else None
    carry_ref = rest.pop(0) if chained else None
    lf = _log_sigmoid(sm_ref[...] + bf_ref[...])
    lf = pltpu.roll(lf, LANE - H_C, axis=1)
    lane = lax.broadcasted_iota(jnp.int32, lf.shape, 1)
    lf = jnp.where(lane < H_C, lf, 0.0)
    cs = _cumsum_rows(lf, seg=C)
    if chained:

        @pl.when(pl.program_id(1) == 0)
        def _():
            carry_ref[...] = jnp.zeros_like(carry_ref)

        cs = carry_ref[...] + cs
        carry_ref[...] = cs[C - 1:C, :]
    if head_major:
        ct_ref[...] = cs.T[0:H_C, :]
    lf_ref[...] = lf[:, 0:H_C]
    c_ref[...] = cs


def _fox_prep(proj, bf, l, *, B, L, row_off, head_major):
    C = math.gcd(L, 4 * CHUNK)
    nch = L // C
    chained = nch > 1
    SB = 1 if (chained or head_major) else _pick(B, (16, 8, 4, 2, 1))
    R = SB * C
    r0 = row_off // R
    out_specs = [pl.BlockSpec((R, H_C), lambda b, n: (b * nch + n, 0)),
                 pl.BlockSpec((R, LANE), lambda b, n: (b * nch + n, 0))]
    out_shape = [jax.ShapeDtypeStruct((B * L, H_C), f32), jax.ShapeDtypeStruct((B * L, LANE), f32)]
    if head_major:
        out_specs.append(pl.BlockSpec((None, H_C, C), lambda b, n: (b, 0, n)))
        out_shape.append(jax.ShapeDtypeStruct((B, H_C, L), f32))
    return pl.pallas_call(
        functools.partial(_fox_prep_kernel, C=C, head_major=head_major, chained=chained),
        grid=(B // SB, nch),
        in_specs=[pl.BlockSpec((R, LANE), lambda b, n: (r0 + b * nch + n, C_SMALL)),
                  pl.BlockSpec((None, 1, LANE), lambda b, n: (l, 0, 0))],
        out_specs=out_specs,
        out_shape=out_shape,
        scratch_shapes=[pltpu.VMEM((1, LANE), f32)] if chained else [],
        compiler_params=_cp("parallel", "arbitrary"),
        name="fox_prep",
    )(proj, bf)


def _fox_prompt_kernel(q_ref, k_ref, v_ref, c_ref, ct_ref, y_ref, kb_ref, vt_ref, qt_ref, m_ref, l_ref,
                       acc_ref, *, tq, tk):
    i = pl.program_id(1)

    @pl.when(i == 0)
    def _():
        kb_ref[...] = k_ref[...].astype(bf16)
        vt_ref[...] = v_ref[...].T.astype(bf16)

    qt_ref[...] = (q_ref[...] * (DH_C ** -0.5)).T.astype(bf16)
    m_ref[...] = jnp.full(m_ref.shape, -jnp.inf, f32)
    l_ref[...] = jnp.zeros(l_ref.shape, f32)
    acc_ref[...] = jnp.zeros(acc_ref.shape, f32)
    q0 = pl.multiple_of(i * tq, tq)
    r = tq // tk
    key_minus_query = (lax.broadcasted_iota(jnp.int32, (tk, tq), 0)
                       - lax.broadcasted_iota(jnp.int32, (tk, tq), 1))

    def block(j, diag):
        start = pl.multiple_of(j * tk, tk)
        for h in range(H_C):
            hs = slice(h * DH_C, (h + 1) * DH_C)
            s = jnp.dot(kb_ref[pl.ds(start, tk), hs], qt_ref[hs, :], preferred_element_type=f32)
            s = s + (ct_ref[h:h + 1, pl.ds(q0, tq)] - c_ref[pl.ds(start, tk), h:h + 1])
            if diag is not None:
                s = jnp.where(key_minus_query <= -diag * tk, s, -jnp.inf)
            m_old = m_ref[h]
            m_new = jnp.maximum(m_old, jnp.max(s, axis=0, keepdims=True))
            p = jnp.exp(s - m_new)
            alpha = jnp.exp(m_old - m_new)
            l_ref[h] = alpha * l_ref[h] + jnp.sum(p, axis=0, keepdims=True)
            acc_ref[hs, :] = alpha * acc_ref[hs, :] + jnp.dot(
                vt_ref[hs, pl.ds(start, tk)], p.astype(bf16), preferred_element_type=f32)
            m_ref[h] = m_new

    def full_block(j, carry):
        block(j, None)
        return carry

    block(i * r, 0)
    lax.fori_loop(0, i * r, full_block, 0)
    for d in range(1, r):
        block(i * r + d, d)
    for h in range(H_C):
        hs = slice(h * DH_C, (h + 1) * DH_C)
        acc_ref[hs, :] = acc_ref[hs, :] / l_ref[h]
    y_ref[...] = acc_ref[...].T.astype(y_ref.dtype)


def _fox_prompt(proj, c, ct, *, B, L, out_dtype, tq_pref=FOX_TQ, tk_pref=FOX_TK):
    tq = _pick(L, tuple(t for t in (2048, 1024, 512, 256, 128, 64, 32, 16, 8) if t <= tq_pref))
    tk = min(tq, tk_pref)
    nq = L // tq
    return pl.pallas_call(
        functools.partial(_fox_prompt_kernel, tq=tq, tk=tk),
        grid=(B, nq),
        in_specs=[pl.BlockSpec((tq, BW), lambda b, i: (b * nq + i, C_CQ)),
                  pl.BlockSpec((L, BW), lambda b, i: (b, C_CK)),
                  pl.BlockSpec((L, BW), lambda b, i: (b, C_CV)),
                  pl.BlockSpec((L, LANE), lambda b, i: (b, 0)),
                  pl.BlockSpec((None, H_C, L), lambda b, i: (b, 0, 0))],
        out_specs=pl.BlockSpec((tq, BW), lambda b, i: (b * nq + i, 0)),
        out_shape=jax.ShapeDtypeStruct((B * L, BW), out_dtype),
        scratch_shapes=[pltpu.VMEM((L, BW), bf16), pltpu.VMEM((BW, L), bf16), pltpu.VMEM((BW, tq), bf16),
                        pltpu.VMEM((H_C, 1, tq), f32), pltpu.VMEM((H_C, 1, tq), f32),
                        pltpu.VMEM((BW, tq), f32)],
        compiler_params=_cp("parallel", "arbitrary"),
        name="fox_prompt",
    )(proj, proj, proj, c, ct)


def _fox_sample_kernel(pt_ref, q_ref, kn_ref, vn_ref, cn_ref, *refs, T, npages):
    kp = refs[0:npages]
    vp = refs[npages:2 * npages]
    lfp = refs[2 * npages:3 * npages]
    y_ref = refs[3 * npages]
    P = npages * PAGE_SIZE
    lf_t = jnp.concatenate([r[...] for r in lfp], axis=1)
    lane = lax.broadcasted_iota(jnp.int32, (H_C, P), 1)
    pre = lf_t
    k = 1
    while k < P:
        pre = pre + jnp.where(lane >= k, pltpu.roll(pre, k, axis=1), 0.0)
        k *= 2
    suffix = pre[:, P - 1:P] - pre
    q = q_ref[...] * (DH_C ** -0.5)
    kn = kn_ref[...]
    vn = vn_ref[...]
    cn_t = cn_ref[...].T[0:H_C, :]
    causal_new = (lax.broadcasted_iota(jnp.int32, (T, T), 1)
                  <= lax.broadcasted_iota(jnp.int32, (T, T), 0))
    for h in range(H_C):
        hs = slice(h * DH_C, (h + 1) * DH_C)
        qh = q[:, hs].astype(bf16)
        k_t = jnp.concatenate([r[h].astype(bf16) for r in kp], axis=1)
        v_t = jnp.concatenate([r[h].astype(bf16) for r in vp], axis=1)
        s = _dot(qh, k_t) + suffix[h:h + 1, :]
        s_new = jnp.where(causal_new, _dot_nt(qh, kn[:, hs]) - cn_t[h:h + 1, :], -jnp.inf)
        m = jnp.maximum(jnp.max(s, axis=-1, keepdims=True), jnp.max(s_new, axis=-1, keepdims=True))
        p = jnp.exp(s - m)
        p_new = jnp.exp(s_new - m)
        denom = jnp.sum(p, axis=-1, keepdims=True) + jnp.sum(p_new, axis=-1, keepdims=True)
        y_ref[:, hs] = ((_dot_nt(p, v_t) + _dot(p_new, vn[:, hs])) / denom).astype(y_ref.dtype)


def _fox_sample(proj, cnew, cache_kt, cache_vt, cache_lft, page_table, l, *, B, T, row_off, out_dtype):
    npages = page_table.shape[1]
    r0 = row_off // T
    kv_specs = [pl.BlockSpec((None, None, H_C, DH_C, PAGE_SIZE), lambda b, pt, p=p: (l, pt[b, p], 0, 0, 0))
                for p in range(npages)]
    lf_specs = [pl.BlockSpec((None, None, H_C, PAGE_SIZE), lambda b, pt, p=p: (l, pt[b, p], 0, 0))
                for p in range(npages)]
    grid_spec = pltpu.PrefetchScalarGridSpec(
        num_scalar_prefetch=1,
        grid=(B,),
        in_specs=[pl.BlockSpec((T, BW), lambda b, pt: (r0 + b, C_CQ)),
                  pl.BlockSpec((T, BW), lambda b, pt: (r0 + b, C_CK)),
                  pl.BlockSpec((T, BW), lambda b, pt: (r0 + b, C_CV)),
                  pl.BlockSpec((T, LANE), lambda b, pt: (b, 0))] + kv_specs + kv_specs + lf_specs,
        out_specs=pl.BlockSpec((T, BW), lambda b, pt: (b, 0)),
    )
    return pl.pallas_call(
        functools.partial(_fox_sample_kernel, T=T, npages=npages),
        grid_spec=grid_spec,
        out_shape=jax.ShapeDtypeStruct((B * T, BW), out_dtype),
        compiler_params=_cp("parallel"),
        name="fox_sample",
    )(page_table, proj, proj, proj, cnew, *([cache_kt] * npages), *([cache_vt] * npages),
      *([cache_lft] * npages))


def _xattn_kernel(q_ref, k_ref, v_ref, o_ref):
    q = q_ref[...]
    dh = q.shape[-1] // H_X
    for h in range(H_X):
        hs = slice(h * dh, (h + 1) * dh)
        s = _dot_nt(q[:, hs], k_ref[:, hs]) * (dh ** -0.5)
        m = jnp.max(s, axis=-1, keepdims=True)
        p = jnp.exp(s - m)
        p = p / jnp.sum(p, axis=-1, keepdims=True)
        o_ref[:, hs] = _dot(p, v_ref[:, hs]).astype(o_ref.dtype)


def _xattn(xq, mem_k, mem_v, lk, lv, *, B, L, row_off, out_dtype):
    D = xq.shape[1]
    n_mem = mem_k.shape[2]
    tq = _pick(L, (512, 256, 128, 64, 32, 16, 8))
    nq = L // tq
    r0 = row_off // tq
    return pl.pallas_call(
        _xattn_kernel,
        grid=(B, nq),
        in_specs=[pl.BlockSpec((tq, D), lambda b, i: (r0 + b * nq + i, 0)),
                  pl.BlockSpec((None, None, n_mem, D), lambda b, i: (lk, b, 0, 0)),
                  pl.BlockSpec((None, None, n_mem, D), lambda b, i: (lv, b, 0, 0))],
        out_specs=pl.BlockSpec((tq, D), lambda b, i: (b * nq + i, 0)),
        out_shape=jax.ShapeDtypeStruct((B * L, D), out_dtype),
        compiler_params=_cp("parallel", "arbitrary"),
        name="xattn_core",
    )(xq, mem_k, mem_v)


def _xattn_decode_kernel(q_ref, k_ref, v_ref, o_ref, *, SB, T):
    n_mem, dh = k_ref.shape[1], k_ref.shape[3]
    R, NK = H_X * T, n_mem * H_X
    same_head = (lax.broadcasted_iota(jnp.int32, (R, NK), 0) // T
                 == lax.broadcasted_iota(jnp.int32, (R, NK), 1) % H_X)
    for s in range(SB):
        rs = slice(s * T, (s + 1) * T)
        q = q_ref[rs, :]
        q_stack = jnp.concatenate([q[:, h * dh:(h + 1) * dh] for h in range(H_X)], axis=0)
        kf = k_ref[s].reshape(NK, dh)
        vf = v_ref[s].reshape(NK, dh)
        sc = jnp.where(same_head, _dot_nt(q_stack, kf) * (dh ** -0.5), -jnp.inf)
        p = jnp.exp(sc - jnp.max(sc, axis=-1, keepdims=True))
        p = p / jnp.sum(p, axis=-1, keepdims=True)
        o = _dot(p, vf)
        for h in range(H_X):
            o_ref[rs, h * dh:(h + 1) * dh] = o[h * T:(h + 1) * T, :].astype(o_ref.dtype)


def _xattn_decode(xq, mem_k, mem_v, l, *, B, T, row_off, out_dtype):
    D = xq.shape[1]
    SB = _pick(B, (4, 2, 1))
    R = SB * T
    r0 = row_off // R
    kv_spec = pl.BlockSpec((None, SB) + mem_k.shape[2:], lambda b: (l, b, 0, 0, 0))
    return pl.pallas_call(
        functools.partial(_xattn_decode_kernel, SB=SB, T=T),
        grid=(B // SB,),
        in_specs=[pl.BlockSpec((R, D), lambda b: (r0 + b, 0)), kv_spec, kv_spec],
        out_specs=pl.BlockSpec((R, D), lambda b: (b, 0)),
        out_shape=jax.ShapeDtypeStruct((B * T, D), out_dtype),
        compiler_params=_cp("parallel"),
        name="xattn_decode",
    )(xq, mem_k, mem_v)


def _rope_tables(pos):
    half = DK_A // 2
    inv_freq = 1.0 / (ROPE_BASE ** (jnp.arange(half, dtype=f32) / half))
    ang = pos.astype(f32)[:, None] * inv_freq[None, :]
    cos, sin = jnp.cos(ang), jnp.sin(ang)
    return (jnp.tile(jnp.concatenate([cos, cos], axis=1), (1, H_A)),
            jnp.tile(jnp.concatenate([-sin, sin], axis=1), (1, H_A)))


def _permute_w_in(w_in):
    sizes = (256, 256, 512, 512, 512, CONV_DIM, H_B, 512, 512, 512, H_C, 256, 256, 512, 512, GLA_RANK)
    offs = [0]
    for s in sizes:
        offs.append(offs[-1] + s)
    (a_q, a_k, a_v, a_g, b_z, b_xbc, b_dt, c_q, c_k, c_v, c_f, d_q, d_k, d_v, d_r, d_lr) = (
        w_in[:, :, offs[i]:offs[i + 1]] for i in range(len(sizes)))
    d = w_in.shape[:2]
    used = 9 * 512 + 5 * 256 + H_B + H_C + GLA_RANK
    cols = [a_v, a_g, b_z, b_xbc[:, :, :BW], c_q, c_k, c_v, d_v, d_r,
            a_q, a_k, b_xbc[:, :, BW:], d_q, d_k,
            b_dt, c_f, d_lr, jnp.zeros(d + (NP_COLS - used,), w_in.dtype)]
    w_mix = jnp.concatenate(cols, axis=2).astype(bf16)
    w_gate = w_in[:, :, offs[-1]:].astype(bf16)
    return w_mix, w_gate


def _pad_lanes(x, start):
    depth, n = x.shape
    return jnp.zeros((depth, 1, LANE), f32).at[:, 0, start:start + n].set(x)


def kernel(x_prompt, x_sample, cache_fox_k, cache_fox_v, cache_fox_logf, cache_mem_k, cache_mem_v, state_ret, state_ssm, state_conv, state_gla, page_table, mem_prompt, w_in, b_gate, ret_norm_g, ssd_conv_w, ssd_conv_b, ssd_dt_bias, ssd_a_log, ssd_d, ssd_norm_g, fox_bf, gla_w_lr2, gla_b_lr, gla_norm_g, w_branch, w_out, norm_mix_g, norm_x_g, norm_mem_g, w_xq, w_mem_kv, w_xo, norm_ffn_g, w_ffn_up, w_ffn_down, norm_final_g):
    Bp, Lp, D = x_prompt.shape
    Bs, Ls, _ = x_sample.shape
    depth = w_in.shape[0]
    n_mem = mem_prompt.shape[1]
    past_len = page_table.shape[1] * PAGE_SIZE
    Tp, Ts = Bp * Lp, Bs * Ls

    w_mix, w_gate = _permute_w_in(w_in)
    w_branch_b = w_branch.astype(bf16)
    w_out_b = w_out.astype(bf16)
    w_xq_b = w_xq.astype(bf16)
    w_xo_b = w_xo.astype(bf16)
    w_kv_b = w_mem_kv.astype(bf16)
    w_up_b = w_ffn_up.astype(bf16)
    w_down_b = w_ffn_down.astype(bf16)
    b_gate4 = b_gate.reshape(depth, N_BRANCH, 1, D)
    g_mix = norm_mix_g.reshape(depth, 1, D)
    g_x = norm_x_g.reshape(depth, 1, D)
    g_mem = norm_mem_g.reshape(depth, 1, D)
    g_ffn = norm_ffn_g.reshape(depth, 1, D)
    ret_g = ret_norm_g.reshape(depth, 1, BW)
    gla_g = gla_norm_g.reshape(depth, 1, BW)
    ssd_g = ssd_norm_g.reshape(depth, 1, BW)
    conv_b = ssd_conv_b.reshape(depth, 1, CONV_DIM)
    dt_bias = _pad_lanes(ssd_dt_bias, 0)
    a_log = _pad_lanes(ssd_a_log, 0)
    d_skip = jnp.repeat(ssd_d, P_B, axis=1).reshape(depth, 1, BW)
    fox_b = _pad_lanes(fox_bf, H_B)
    w_lr2 = jnp.zeros((depth, LANE, H_D * DK_D), f32).at[:, H_B + H_C:H_B + H_C + GLA_RANK, :].set(
        gla_w_lr2).astype(bf16)
    b_lr = gla_b_lr.reshape(depth, 1, H_D * DK_D)
    cos_p, sin_p = _rope_tables(jnp.arange(Lp, dtype=jnp.int32))
    cos_s, sin_s = _rope_tables(past_len + jnp.arange(Ls, dtype=jnp.int32))
    ck_t = jnp.transpose(cache_fox_k, (0, 1, 3, 4, 2))
    cv_t = jnp.transpose(cache_fox_v, (0, 1, 3, 4, 2))
    clf_t = jnp.transpose(cache_fox_logf, (0, 1, 3, 2))
    z_ret = jnp.zeros((1, Bp, H_A, DK_A, DV_A), f32)
    z_ssm = jnp.zeros((1, Bp, H_B, N_B, P_B), f32)
    z_conv = jnp.zeros((1, Bp, CONV_W - 1, CONV_DIM), f32)
    z_gla = jnp.zeros((1, Bp, H_D, DK_D, DV_D), f32)
    mem2d = mem_prompt.reshape(Bp * n_mem, D)

    h = jnp.concatenate([x_prompt.reshape(Tp, D), x_sample.reshape(Ts, D)], axis=0)
    outs = {k: [] for k in ("ret_p", "ssm_p", "conv_p", "gla_p", "fk_p", "fv_p", "flf_p", "mk_p", "mv_p",
                            "ret_s", "ssm_s", "conv_s", "gla_s", "fk_s", "fv_s", "flf_s")}
    for l in range(depth):
        proj = _norm_matmul(h, g_mix, w_mix, l, tn=1024)

        pk = dict(B=Bp, L=Lp, row_off=0, out_dtype=bf16)
        sk = dict(B=Bs, L=Ls, row_off=Tp, out_dtype=f32)
        ya_p, ret_p = _retention(proj, cos_p, sin_p, z_ret, 0, ret_g, l, **pk)
        ya_s, ret_s = _retention(proj, cos_s, sin_s, state_ret, l, ret_g, l, **sk)
        yb_p, conv_p, ssm_p = _ssd(proj, z_conv, z_ssm, 0, ssd_conv_w, conv_b, dt_bias, a_log, d_skip,
                                   ssd_g, l, **pk)
        yb_s, conv_s, ssm_s = _ssd(proj, state_conv, state_ssm, l, ssd_conv_w, conv_b, dt_bias, a_log,
                                   d_skip, ssd_g, l, **sk)
        lf_p, c_p, ct_p = _fox_prep(proj, fox_b, l, B=Bp, L=Lp, row_off=0, head_major=True)
        lf_s, c_s = _fox_prep(proj, fox_b, l, B=Bs, L=Ls, row_off=Tp, head_major=False)
        yc_p = _fox_prompt(proj, c_p, ct_p, B=Bp, L=Lp, out_dtype=bf16)
        yc_s = _fox_sample(proj, c_s, ck_t, cv_t, clf_t, page_table, l, B=Bs, T=Ls, row_off=Tp, out_dtype=f32)
        yd_p, gla_p = _gla(proj, z_gla, 0, w_lr2, b_lr, gla_g, l, **pk)
        yd_s, gla_s = _gla(proj, state_gla, l, w_lr2, b_lr, gla_g, l, **sk)

        h = _merge(h, g_mix, (ya_p, yb_p, yc_p, yd_p), (ya_s, yb_s, yc_s, yd_s),
                   w_gate, b_gate4, w_branch_b, w_out_b, l)

        kv = _norm_matmul(mem2d, g_mem, w_kv_b, l, tn=D, split=True)
        mem_kv = kv.reshape(2, Bp, n_mem, D)
        xq = _norm_matmul(h, g_x, w_xq_b, l, tn=D)
        o_p = _xattn(xq, mem_kv, mem_kv, 0, 1, B=Bp, L=Lp, row_off=0, out_dtype=bf16)
        o_s = _xattn_decode(xq, cache_mem_k, cache_mem_v, l, B=Bs, T=Ls, row_off=Tp, out_dtype=f32)
        h = _xo(h, o_p, o_s, w_xo_b, l)
        h = _ffn(h, g_ffn, w_up_b, w_down_b, l)

        ckv = proj[:, C_CK * BW:(C_CV + 1) * BW]
        outs["ret_p"].append(ret_p); outs["ssm_p"].append(ssm_p); outs["conv_p"].append(conv_p)
        outs["gla_p"].append(gla_p)
        outs["fk_p"].append(ckv[:Tp, :BW].reshape(Bp, Lp, H_C, DH_C))
        outs["fv_p"].append(ckv[:Tp, BW:].reshape(Bp, Lp, H_C, DH_C))
        outs["flf_p"].append(lf_p.reshape(Bp, Lp, H_C))
        outs["mk_p"].append(kv[0].reshape(Bp, n_mem, H_X, D // H_X))
        outs["mv_p"].append(kv[1].reshape(Bp, n_mem, H_X, D // H_X))
        outs["ret_s"].append(ret_s); outs["ssm_s"].append(ssm_s); outs["conv_s"].append(conv_s)
        outs["gla_s"].append(gla_s)
        outs["fk_s"].append(ckv[Tp:, :BW].reshape(Bs, Ls, H_C, DH_C))
        outs["fv_s"].append(ckv[Tp:, BW:].reshape(Bs, Ls, H_C, DH_C))
        outs["flf_s"].append(lf_s.reshape(Bs, Ls, H_C))

    y = _final_norm(h, norm_final_g.reshape(1, D))
    st = {k: jnp.stack(v, axis=0) for k, v in outs.items()}
    return (y[:Tp].reshape(Bp, Lp, D), y[Tp:].reshape(Bs, Ls, D),
            st["ret_p"], st["ssm_p"], st["conv_p"], st["gla_p"], st["fk_p"], st["fv_p"], st["flf_p"],
            st["mk_p"], st["mv_p"],
            st["ret_s"], st["ssm_s"], st["conv_s"], st["gla_s"], st["fk_s"], st["fv_s"], st["flf_s"])
```

```python
import functools
import math

import jax
import jax.numpy as jnp
from jax import lax
from jax.experimental import pallas as pl
from jax.experimental.pallas import tpu as pltpu

f32 = jnp.float32
bf16 = jnp.bfloat16

EPS = 1e-6
CHUNK = 128
GLA_CHUNK = 16
GLA_SUB = 16
CHUNKS_PER_STEP = 4
FOX_TQ, FOX_TK = 512, 512
ROPE_BASE = 10000.0
PAGE_SIZE = 128
H_A, DK_A, DV_A = 4, 64, 128
H_B, P_B, N_B, G_B, CONV_W = 8, 64, 64, 2, 4
H_C, DH_C = 8, 64
H_D, DK_D, DV_D = 4, 64, 128
GLA_RANK = 16
GLA_TAU = 16.0
N_BRANCH = 4
H_X = 4
BW = 512
CONV_DIM = BW + 2 * G_B * N_B

C_AV, C_AG, C_BZ, C_BXS, C_CQ, C_CK, C_CV, C_DV, C_DR = range(9)
C_AQ, C_AK, C_BBC, C_DQ, C_DK = range(18, 23)
C_SMALL = 46
NP_COLS = 6144
LANE = 128
VMEM_LIMIT = 56 * 1024 * 1024


def _cp(*sem):
    return pltpu.CompilerParams(dimension_semantics=sem, vmem_limit_bytes=VMEM_LIMIT)


def _pick(n, cands):
    for c in cands:
        if n % c == 0:
            return c
    raise ValueError(f"no tile for {n}")


def _dot(a, b):
    return jnp.dot(a.astype(bf16), b.astype(bf16), preferred_element_type=f32)


def _dot_nt(a, b):
    return lax.dot_general(a.astype(bf16), b.astype(bf16), (((1,), (1,)), ((), ())),
                           preferred_element_type=f32)


def _dot_tn(a, b):
    return lax.dot_general(a.astype(bf16), b.astype(bf16), (((0,), (0,)), ((), ())),
                           preferred_element_type=f32)


def _silu(x):
    return x * jax.nn.sigmoid(x)


def _log_sigmoid(x):
    return jnp.minimum(x, 0.0) - jnp.log1p(jnp.exp(-jnp.abs(x)))


def _softplus(x):
    return jnp.maximum(x, 0.0) + jnp.log1p(jnp.exp(-jnp.abs(x)))


def _cumsum_rows(x, seg=None):
    n = x.shape[0] if seg is None else seg
    row = lax.broadcasted_iota(jnp.int32, x.shape, 0)
    if n != x.shape[0]:
        row = row % n
    k = 1
    while k < n:
        x = x + jnp.where(row >= k, pltpu.roll(x, k, axis=0), 0.0)
        k *= 2
    return x


def _rms(x, g):
    return x * lax.rsqrt(jnp.mean(x * x, axis=-1, keepdims=True) + EPS) * g


def _norm_matmul_kernel(x_ref, g_ref, w_ref, o_ref, xn_ref):
    @pl.when(pl.program_id(1) == 0)
    def _():
        xn_ref[...] = _rms(x_ref[...], g_ref[...]).astype(bf16)

    o_ref[...] = jnp.dot(xn_ref[...], w_ref[...], preferred_element_type=f32).astype(o_ref.dtype)


def _norm_matmul(x, g, w, l, *, tn, out_dtype=f32, split=False):
    M, K = x.shape
    N = w.shape[2]
    tm = _pick(M, (1024, 512, 256, 128, 64, 32, 16, 8))
    if split:
        out_shape = jax.ShapeDtypeStruct((N // tn, M, tn), out_dtype)
        out_spec = pl.BlockSpec((None, tm, tn), lambda i, j: (j, i, 0))
    else:
        out_shape = jax.ShapeDtypeStruct((M, N), out_dtype)
        out_spec = pl.BlockSpec((tm, tn), lambda i, j: (i, j))
    return pl.pallas_call(
        _norm_matmul_kernel,
        grid=(M // tm, N // tn),
        in_specs=[pl.BlockSpec((tm, K), lambda i, j: (i, 0)),
                  pl.BlockSpec((None, 1, K), lambda i, j: (l, 0, 0)),
                  pl.BlockSpec((None, K, tn), lambda i, j: (l, 0, j))],
        out_specs=out_spec,
        out_shape=out_shape,
        scratch_shapes=[pltpu.VMEM((tm, K), bf16)],
        compiler_params=_cp("parallel", "arbitrary"),
        name="norm_matmul",
    )(x, g, w)


def _two_src(tm, width, n_p):
    return (pl.BlockSpec((tm, width), lambda m: (jnp.minimum(m, n_p - 1), 0)),
            pl.BlockSpec((tm, width), lambda m: (jnp.maximum(m - n_p, 0), 0)))


def _merge_kernel(h_ref, g_ref, *refs, n_p):
    yp = refs[0:N_BRANCH]
    ys = refs[N_BRANCH:2 * N_BRANCH]
    wg_ref, bg_ref, wb_ref, wo_ref, o_ref = refs[2 * N_BRANCH:]
    m = pl.program_id(0)
    h = h_ref[...]
    D = h.shape[1]
    xn = _rms(h, g_ref[...]).astype(bf16)
    merged = None
    for k in range(N_BRANCH):
        gate = jax.nn.sigmoid(
            jnp.dot(xn, wg_ref[:, k * D:(k + 1) * D], preferred_element_type=f32) + bg_ref[k])
        y = jnp.where(m < n_p, yp[k][...].astype(bf16), ys[k][...].astype(bf16))
        term = gate * jnp.dot(y, wb_ref[k], preferred_element_type=f32)
        merged = term if merged is None else merged + term
    o_ref[...] = h + jnp.dot(merged.astype(bf16), wo_ref[...], preferred_element_type=f32)


def _merge(h, g, yps, yss, w_gate, b_gate, w_branch, w_out, l):
    T, D = h.shape
    Tp, Ts = yps[0].shape[0], yss[0].shape[0]
    tm = _pick(math.gcd(Tp, Ts), (512, 256, 128, 64, 32, 16, 8))
    n_p = Tp // tm
    sp, ss = _two_src(tm, BW, n_p)
    resident = lambda shape: pl.BlockSpec((None,) + shape, lambda m: (l,) + (0,) * len(shape),
                                          pipeline_mode=pl.Buffered(1))
    return pl.pallas_call(
        functools.partial(_merge_kernel, n_p=n_p),
        grid=(T // tm,),
        in_specs=[pl.BlockSpec((tm, D), lambda m: (m, 0)),
                  pl.BlockSpec((None, 1, D), lambda m: (l, 0, 0))]
                 + [sp] * N_BRANCH + [ss] * N_BRANCH
                 + [resident((D, N_BRANCH * D)), resident((N_BRANCH, 1, D)), resident((N_BRANCH, BW, D)),
                    resident((D, D))],
        out_specs=pl.BlockSpec((tm, D), lambda m: (m, 0)),
        out_shape=jax.ShapeDtypeStruct((T, D), f32),
        compiler_params=_cp("parallel"),
        name="branch_merge",
    )(h, g, *yps, *yss, w_gate, b_gate, w_branch, w_out)


def _xo_kernel(h_ref, op_ref, os_ref, w_ref, o_ref, *, n_p):
    m = pl.program_id(0)
    o = jnp.where(m < n_p, op_ref[...].astype(bf16), os_ref[...].astype(bf16))
    o_ref[...] = h_ref[...] + jnp.dot(o, w_ref[...], preferred_element_type=f32)


def _xo(h, o_p, o_s, w, l):
    T, D = h.shape
    Tp, Ts = o_p.shape[0], o_s.shape[0]
    tm = _pick(math.gcd(Tp, Ts), (1024, 512, 256, 128, 64, 32, 16, 8))
    n_p = Tp // tm
    sp, ss = _two_src(tm, D, n_p)
    return pl.pallas_call(
        functools.partial(_xo_kernel, n_p=n_p),
        grid=(T // tm,),
        in_specs=[pl.BlockSpec((tm, D), lambda m: (m, 0)), sp, ss,
                  pl.BlockSpec((None, D, D), lambda m: (l, 0, 0))],
        out_specs=pl.BlockSpec((tm, D), lambda m: (m, 0)),
        out_shape=jax.ShapeDtypeStruct((T, D), f32),
        compiler_params=_cp("parallel"),
        name="xattn_out",
    )(h, o_p, o_s, w)


def _ffn_kernel(h_ref, g_ref, wa_ref, wb_ref, wd_ref, o_ref, xn_ref, acc_ref, *, nj):
    j = pl.program_id(1)

    @pl.when(j == 0)
    def _():
        xn_ref[...] = _rms(h_ref[...], g_ref[...]).astype(bf16)
        acc_ref[...] = jnp.zeros_like(acc_ref)

    xn = xn_ref[...]
    a = jnp.dot(xn, wa_ref[...], preferred_element_type=f32)
    b = jnp.dot(xn, wb_ref[...], preferred_element_type=f32)
    acc_ref[...] += jnp.dot((_silu(a) * b).astype(bf16), wd_ref[...], preferred_element_type=f32)

    @pl.when(j == nj - 1)
    def _():
        o_ref[...] = h_ref[...] + acc_ref[...]


def _ffn(h, g, w_up, w_down, l):
    T, D = h.shape
    d_ff = w_down.shape[1]
    tm = _pick(T, (512, 256, 128, 64, 32, 16, 8))
    tf = d_ff // 2 if d_ff % (2 * LANE) == 0 else _pick(d_ff, (256, 128))
    nj = d_ff // tf
    return pl.pallas_call(
        functools.partial(_ffn_kernel, nj=nj),
        grid=(T // tm, nj),
        in_specs=[pl.BlockSpec((tm, D), lambda m, j: (m, 0)),
                  pl.BlockSpec((None, 1, D), lambda m, j: (l, 0, 0)),
                  pl.BlockSpec((None, D, tf), lambda m, j: (l, 0, j)),
                  pl.BlockSpec((None, D, tf), lambda m, j: (l, 0, j + nj)),
                  pl.BlockSpec((None, tf, D), lambda m, j: (l, j, 0))],
        out_specs=pl.BlockSpec((tm, D), lambda m, j: (m, 0)),
        out_shape=jax.ShapeDtypeStruct((T, D), f32),
        scratch_shapes=[pltpu.VMEM((tm, D), bf16), pltpu.VMEM((tm, D), f32)],
        compiler_params=_cp("parallel", "arbitrary"),
        name="swiglu_ffn",
    )(h, g, w_up, w_up, w_down)


def _final_norm_kernel(x_ref, g_ref, o_ref):
    o_ref[...] = _rms(x_ref[...], g_ref[...])


def _final_norm(h, g):
    T, D = h.shape
    tm = _pick(T, (1024, 512, 256, 128, 64, 32, 16, 8))
    return pl.pallas_call(
        _final_norm_kernel,
        grid=(T // tm,),
        in_specs=[pl.BlockSpec((tm, D), lambda m: (m, 0)), pl.BlockSpec((1, D), lambda m: (0, 0))],
        out_specs=pl.BlockSpec((tm, D), lambda m: (m, 0)),
        out_shape=jax.ShapeDtypeStruct((T, D), f32),
        compiler_params=_cp("parallel"),
        name="final_norm",
    )(h, g)


def _units_per_step(B, nchunks):
    if nchunks == 1:
        return _pick(B, (8, 4, 2, 1)), 1
    return 1, _pick(nchunks, (CHUNKS_PER_STEP, 2, 1))


def _ret_kernel(q_ref, k_ref, v_ref, g_ref, cos_ref, sin_ref, s0_ref, ng_ref, y_ref, sf_ref, S_ref,
                *, C, nch, SB, CB):
    n = pl.program_id(1)

    @pl.when(n == 0)
    def _():
        S_ref[...] = s0_ref[...]

    lane = lax.broadcasted_iota(jnp.int32, (C, H_A * DK_A), 1)
    first_half = (lane % DK_A) < (DK_A // 2)

    def rot(x, cos, sin):
        fwd = pltpu.roll(x, DK_A // 2, axis=1)
        bwd = pltpu.roll(x, H_A * DK_A - DK_A // 2, axis=1)
        return x * cos + jnp.where(first_half, bwd, fwd) * sin

    ng = ng_ref[...]
    t_col = lax.broadcasted_iota(jnp.int32, (C, 1), 0).astype(f32)
    diff = (lax.broadcasted_iota(jnp.int32, (C, C), 0)
            - lax.broadcasted_iota(jnp.int32, (C, C), 1))
    for u in range(SB * CB):
        s, cb = divmod(u, CB)
        rs = slice(u * C, (u + 1) * C)
        cos = cos_ref[cb * C:(cb + 1) * C, :]
        sin = sin_ref[cb * C:(cb + 1) * C, :]
        q = rot(q_ref[rs, :], cos, sin)
        k = rot(k_ref[rs, :], cos, sin) * (DK_A ** -0.5)
        v = v_ref[rs, :]
        gate = g_ref[rs, :]
        for h in range(H_A):
            lg = math.log1p(-2.0 ** (-5.0 - h))
            qh = q[:, h * DK_A:(h + 1) * DK_A]
            kh = k[:, h * DK_A:(h + 1) * DK_A]
            vs = slice(h * DV_A, (h + 1) * DV_A)
            vh = v[:, vs]
            decay = jnp.where(diff >= 0, jnp.exp(diff.astype(f32) * lg), 0.0)
            scores = _dot_nt(qh, kh) * decay
            s_prev = S_ref[s, h]
            o = _dot(scores, vh) + _dot(qh * jnp.exp((t_col + 1.0) * lg), s_prev)
            k_out = kh * jnp.exp((C - 1.0 - t_col) * lg)
            S_ref[s, h] = math.exp(C * lg) * s_prev + _dot_tn(k_out, vh)
            o = o - jnp.mean(o, axis=-1, keepdims=True)
            o = o * lax.rsqrt(jnp.mean(o * o, axis=-1, keepdims=True) + EPS)
            y_ref[rs, vs] = (_silu(gate[:, vs]) * (o * ng[:, vs])).astype(y_ref.dtype)

    @pl.when(n == nch - 1)
    def _():
        sf_ref[...] = S_ref[...]


def _retention(proj, cos, sin, s0, ls, ng, l, *, B, L, row_off, out_dtype):
    C = math.gcd(L, CHUNK)
    SB, CB = _units_per_step(B, L // C)
    nch = L // (C * CB)
    R = SB * CB * C
    r0 = row_off // R
    row = lambda b, n: r0 + b * nch + n
    return pl.pallas_call(
        functools.partial(_ret_kernel, C=C, nch=nch, SB=SB, CB=CB),
        grid=(B // SB, nch),
        in_specs=[pl.BlockSpec((R, 256), lambda b, n: (row(b, n), C_AQ)),
                  pl.BlockSpec((R, 256), lambda b, n: (row(b, n), C_AK)),
                  pl.BlockSpec((R, 512), lambda b, n: (row(b, n), C_AV)),
                  pl.BlockSpec((R, 512), lambda b, n: (row(b, n), C_AG)),
                  pl.BlockSpec((CB * C, 256), lambda b, n: (n, 0)),
                  pl.BlockSpec((CB * C, 256), lambda b, n: (n, 0)),
                  pl.BlockSpec((None, SB, H_A, DK_A, DV_A), lambda b, n: (ls, b, 0, 0, 0)),
                  pl.BlockSpec((None, 1, BW), lambda b, n: (l, 0, 0))],
        out_specs=[pl.BlockSpec((R, BW), lambda b, n: (b * nch + n, 0)),
                   pl.BlockSpec((SB, H_A, DK_A, DV_A), lambda b, n: (b, 0, 0, 0))],
        out_shape=[jax.ShapeDtypeStruct((B * L, BW), out_dtype),
                   jax.ShapeDtypeStruct((B, H_A, DK_A, DV_A), f32)],
        scratch_shapes=[pltpu.VMEM((SB, H_A, DK_A, DV_A), f32)],
        compiler_params=_cp("parallel", "arbitrary"),
        name="retention",
    )(proj, proj, proj, proj, cos, sin, s0, ng)


def _ssd_kernel(z_ref, xs_ref, bc_ref, sm_ref, c0_ref, s0_ref, cw_ref, cb_ref, dtb_ref, al_ref,
                dsk_ref, ng_ref, y_ref, cn_ref, sf_ref, xpad_ref, S_ref, o_ref, *, C, nch, SB, CB):
    n = pl.program_id(1)
    P0 = 8 - (CONV_W - 1)

    @pl.when(n == 0)
    def _():
        xpad_ref[:, P0:8, :] = c0_ref[...]
        S_ref[...] = s0_ref[...]

    cw = cw_ref[...]
    dsk = dsk_ref[...]
    causal = (lax.broadcasted_iota(jnp.int32, (C, C), 0)
              >= lax.broadcasted_iota(jnp.int32, (C, C), 1))
    rep = H_B // G_B
    for u in range(SB * CB):
        s = u // CB
        rs = slice(u * C, (u + 1) * C)
        xpad_ref[s, 8:8 + C, 0:BW] = xs_ref[rs, :]
        xpad_ref[s, 8:8 + C, BW:CONV_DIM] = bc_ref[rs, :]
        conv = cb_ref[...] + xpad_ref[s, P0:P0 + C, :] * cw[0:1, :]
        for j in range(1, CONV_W):
            conv = conv + xpad_ref[s, P0 + j:P0 + j + C, :] * cw[j:j + 1, :]
        tail = xpad_ref[s, C + P0:C + 8, :]
        cn_ref[s] = tail
        xpad_ref[s, P0:8, :] = tail

        act = _silu(conv)
        xs = act[:, 0:BW]
        dt = _softplus(sm_ref[rs, :] + dtb_ref[...])
        la = dt * (-jnp.exp(al_ref[...]))
        cum = _cumsum_rows(la)
        cum_t = cum.T
        cum_last = cum[C - 1:C, :]
        for g in range(G_B):
            bm = act[:, BW + g * N_B:BW + (g + 1) * N_B]
            cm = act[:, BW + G_B * N_B + g * N_B:BW + G_B * N_B + (g + 1) * N_B]
            scores_g = _dot_nt(cm, bm)
            for h in range(g * rep, (g + 1) * rep):
                ps = slice(h * P_B, (h + 1) * P_B)
                cum_h = cum[:, h:h + 1]
                seg = cum_h - cum_t[h:h + 1, :]
                decay = jnp.exp(jnp.where(causal, seg, -jnp.inf))
                xh = xs[:, ps]
                vh = xh * dt[:, h:h + 1]
                s_prev = S_ref[s, h]
                o = _dot(scores_g * decay, vh) + _dot(cm * jnp.exp(cum_h), s_prev)
                k_out = bm * jnp.exp(cum_last[:, h:h + 1] - cum_h)
                S_ref[s, h] = jnp.exp(cum_last[:, h:h + 1]) * s_prev + _dot_tn(k_out, vh)
                o_ref[u, :, ps] = o + dsk[:, ps] * xh
        y = o_ref[u] * _silu(z_ref[rs, :])
        y_ref[rs, :] = _rms(y, ng_ref[...]).astype(y_ref.dtype)

    @pl.when(n == nch - 1)
    def _():
        sf_ref[...] = S_ref[...]


def _ssd(proj, c0, s0, ls, cw, cb, dtb, al, dsk, ng, l, *, B, L, row_off, out_dtype):
    C = math.gcd(L, CHUNK)
    SB, CB = _units_per_step(B, L // C)
    nch = L // (C * CB)
    R = SB * CB * C
    r0 = row_off // R
    row = lambda b, n: r0 + b * nch + n
    par = lambda shape: pl.BlockSpec((None,) + shape, lambda b, n: (l,) + (0,) * len(shape))
    return pl.pallas_call(
        functools.partial(_ssd_kernel, C=C, nch=nch, SB=SB, CB=CB),
        grid=(B // SB, nch),
        in_specs=[pl.BlockSpec((R, 512), lambda b, n: (row(b, n), C_BZ)),
                  pl.BlockSpec((R, 512), lambda b, n: (row(b, n), C_BXS)),
                  pl.BlockSpec((R, 256), lambda b, n: (row(b, n), C_BBC)),
                  pl.BlockSpec((R, 128), lambda b, n: (row(b, n), C_SMALL)),
                  pl.BlockSpec((None, SB, CONV_W - 1, CONV_DIM), lambda b, n: (ls, b, 0, 0)),
                  pl.BlockSpec((None, SB, H_B, N_B, P_B), lambda b, n: (ls, b, 0, 0, 0)),
                  par((CONV_W, CONV_DIM)), par((1, CONV_DIM)), par((1, LANE)), par((1, LANE)),
                  par((1, BW)), par((1, BW))],
        out_specs=[pl.BlockSpec((R, BW), lambda b, n: (b * nch + n, 0)),
                   pl.BlockSpec((SB, CONV_W - 1, CONV_DIM), lambda b, n: (b, 0, 0)),
                   pl.BlockSpec((SB, H_B, N_B, P_B), lambda b, n: (b, 0, 0, 0))],
        out_shape=[jax.ShapeDtypeStruct((B * L, BW), out_dtype),
                   jax.ShapeDtypeStruct((B, CONV_W - 1, CONV_DIM), f32),
                   jax.ShapeDtypeStruct((B, H_B, N_B, P_B), f32)],
        scratch_shapes=[pltpu.VMEM((SB, C + 8, CONV_DIM), f32), pltpu.VMEM((SB, H_B, N_B, P_B), f32),
                        pltpu.VMEM((SB * CB, C, BW), f32)],
        compiler_params=_cp("parallel", "arbitrary"),
        name="ssd",
    )(proj, proj, proj, proj, c0, s0, cw, cb, dtb, al, dsk, ng)


def _gla_kernel(q_ref, k_ref, v_ref, r_ref, sm_ref, s0_ref, w2_ref, b2_ref, ng_ref, y_ref, sf_ref,
                S_ref, *, C, c, nch, SB, CB):
    n = pl.program_id(1)

    @pl.when(n == 0)
    def _():
        S_ref[...] = s0_ref[...]

    x_all = jnp.dot(sm_ref[...].astype(bf16), w2_ref[...], preferred_element_type=f32) + b2_ref[...]
    la_all = _log_sigmoid(x_all) / GLA_TAU
    for u in range(SB * CB):
        rs = slice(u * C, (u + 1) * C)
        _gla_one_chunk(la_all[rs, :], q_ref[rs, :], k_ref[rs, :], v_ref[rs, :], r_ref[rs, :], ng_ref[...],
                       y_ref, rs, S_ref, u // CB, C=C, c=c)

    @pl.when(n == nch - 1)
    def _():
        sf_ref[...] = S_ref[...]


def _gla_one_chunk(la, q, k, v, r, ng, y_ref, rs, S_ref, s, *, C, c):
    nsub = C // c
    HK = H_D * DK_D
    cum = _cumsum_rows(la)
    cum_last = cum[C - 1:C, :]
    qs = q * (DK_D ** -0.5)
    q_in = qs * jnp.exp(cum)
    k_out = k * jnp.exp(cum_last - cum)
    dec_col = jnp.broadcast_to(jnp.exp(cum_last), (8, HK)).T[:, 0:1]

    row = lax.broadcasted_iota(jnp.int32, (C, LANE), 0)
    lane = lax.broadcasted_iota(jnp.int32, (C, LANE), 1)
    causal = (lax.broadcasted_iota(jnp.int32, (C, C), 0)
              >= lax.broadcasted_iota(jnp.int32, (C, C), 1))
    for p in range(HK // LANE):
        sl = slice(p * LANE, (p + 1) * LANE)
        cum_p = cum[:, sl]
        k_p = k[:, sl]
        ref_row = jnp.zeros((C, LANE), f32)
        k_pieces = []
        for i in range(nsub):
            ref_i = cum_p[i * c - 1:i * c, :] if i > 0 else jnp.zeros((1, LANE), f32)
            in_blk = (row >= i * c) & (row < (i + 1) * c)
            ref_row = jnp.where(in_blk, ref_i, ref_row)
            expo = jnp.where(row < (i + 1) * c, ref_i - cum_p, -jnp.inf)
            k_pieces.append((k_p * jnp.exp(expo)).astype(bf16))
        k_big = jnp.concatenate(k_pieces, axis=1) if nsub > 1 else k_pieces[0]
        q_sc = qs[:, sl] * jnp.exp(cum_p - ref_row)
        for hh in range(LANE // DK_D):
            h = p * (LANE // DK_D) + hh
            in_head = (lane >= hh * DK_D) & (lane < (hh + 1) * DK_D)
            q_pieces = []
            for i in range(nsub):
                in_blk = (row >= i * c) & (row < (i + 1) * c)
                q_pieces.append(jnp.where(in_blk & in_head, q_sc, 0.0).astype(bf16))
            q_big = jnp.concatenate(q_pieces, axis=1) if nsub > 1 else q_pieces[0]
            scores = jnp.where(causal, _dot_nt(q_big, k_big), 0.0)
            vs = slice(h * DV_D, (h + 1) * DV_D)
            vh = v[:, vs]
            s_prev = S_ref[s, h]
            hs = slice(h * DK_D, (h + 1) * DK_D)
            o = _dot(scores, vh) + _dot(q_in[:, hs], s_prev)
            S_ref[s, h] = dec_col[hs, :] * s_prev + _dot_tn(k_out[:, hs], vh)
            o = o * lax.rsqrt(jnp.mean(o * o, axis=-1, keepdims=True) + EPS)
            y_ref[rs, vs] = (_silu(r[:, vs]) * (o * ng[:, vs])).astype(y_ref.dtype)


def _gla(proj, s0, ls, w2, b2, ng, l, *, B, L, row_off, out_dtype):
    C = math.gcd(L, CHUNK)
    c = math.gcd(L, GLA_SUB)
    SB, CB = _units_per_step(B, L // C)
    nch = L // (C * CB)
    R = SB * CB * C
    r0 = row_off // R
    row = lambda b, n: r0 + b * nch + n
    par = lambda shape: pl.BlockSpec((None,) + shape, lambda b, n: (l,) + (0,) * len(shape))
    return pl.pallas_call(
        functools.partial(_gla_kernel, C=C, c=c, nch=nch, SB=SB, CB=CB),
        grid=(B // SB, nch),
        in_specs=[pl.BlockSpec((R, 256), lambda b, n: (row(b, n), C_DQ)),
                  pl.BlockSpec((R, 256), lambda b, n: (row(b, n), C_DK)),
                  pl.BlockSpec((R, 512), lambda b, n: (row(b, n), C_DV)),
                  pl.BlockSpec((R, 512), lambda b, n: (row(b, n), C_DR)),
                  pl.BlockSpec((R, 128), lambda b, n: (row(b, n), C_SMALL)),
                  pl.BlockSpec((None, SB, H_D, DK_D, DV_D), lambda b, n: (ls, b, 0, 0, 0)),
                  par((LANE, H_D * DK_D)), par((1, H_D * DK_D)), par((1, BW))],
        out_specs=[pl.BlockSpec((R, BW), lambda b, n: (b * nch + n, 0)),
                   pl.BlockSpec((SB, H_D, DK_D, DV_D), lambda b, n: (b, 0, 0, 0))],
        out_shape=[jax.ShapeDtypeStruct((B * L, BW), out_dtype),
                   jax.ShapeDtypeStruct((B, H_D, DK_D, DV_D), f32)],
        scratch_shapes=[pltpu.VMEM((SB, H_D, DK_D, DV_D), f32)],
        compiler_params=_cp("parallel", "arbitrary"),
        name="gla",
    )(proj, proj, proj, proj, proj, s0, w2, b2, ng)


def _fox_prep_kernel(sm_ref, bf_ref, lf_ref, c_ref, *rest, C, head_major, chained):
    rest = list(rest)
    ct_ref = rest.pop(0) if head_major else None
    carry_ref = rest.pop(0) if chained else None
    lf = _log_sigmoid(sm_ref[...] + bf_ref[...])
    lf = pltpu.roll(lf, LANE - H_C, axis=1)
    lane = lax.broadcasted_iota(jnp.int32, lf.shape, 1)
    lf = jnp.where(lane < H_C, lf, 0.0)
    cs = _cumsum_rows(lf, seg=C)
    if chained:

        @pl.when(pl.program_id(1) == 0)
        def _():
            carry_ref[...] = jnp.zeros_like(carry_ref)

        cs = carry_ref[...] + cs
        carry_ref[...] = cs[C - 1:C, :]
    if head_major:
        ct_ref[...] = cs.T[0:H_C, :]
    lf_ref[...] = lf[:, 0:H_C]
    c_ref[...] = cs


def _fox_prep(proj, bf, l, *, B, L, row_off, head_major):
    C = math.gcd(L, 4 * CHUNK)
    nch = L // C
    chained = nch > 1
    SB = 1 if (chained or head_major) else _pick(B, (16, 8, 4, 2, 1))
    R = SB * C
    r0 = row_off // R
    out_specs = [pl.BlockSpec((R, H_C), lambda b, n: (b * nch + n, 0)),
                 pl.BlockSpec((R, LANE), lambda b, n: (b * nch + n, 0))]
    out_shape = [jax.ShapeDtypeStruct((B * L, H_C), f32), jax.ShapeDtypeStruct((B * L, LANE), f32)]
    if head_major:
        out_specs.append(pl.BlockSpec((None, H_C, C), lambda b, n: (b, 0, n)))
        out_shape.append(jax.ShapeDtypeStruct((B, H_C, L), f32))
    return pl.pallas_call(
        functools.partial(_fox_prep_kernel, C=C, head_major=head_major, chained=chained),
        grid=(B // SB, nch),
        in_specs=[pl.BlockSpec((R, LANE), lambda b, n: (r0 + b * nch + n, C_SMALL)),
                  pl.BlockSpec((None, 1, LANE), lambda b, n: (l, 0, 0))],
        out_specs=out_specs,
        out_shape=out_shape,
        scratch_shapes=[pltpu.VMEM((1, LANE), f32)] if chained else [],
        compiler_params=_cp("parallel", "arbitrary"),
        name="fox_prep",
    )(proj, bf)


def _fox_prompt_kernel(q_ref, k_ref, v_ref, c_ref, ct_ref, y_ref, kb_ref, vt_ref, qt_ref, m_ref, l_ref,
                       acc_ref, *, tq, tk):
    i = pl.program_id(1)

    @pl.when(i == 0)
    def _():
        kb_ref[...] = k_ref[...].astype(bf16)
        vt_ref[...] = v_ref[...].T.astype(bf16)

    qt_ref[...] = (q_ref[...] * (DH_C ** -0.5)).T.astype(bf16)
    m_ref[...] = jnp.full(m_ref.shape, -jnp.inf, f32)
    l_ref[...] = jnp.zeros(l_ref.shape, f32)
    acc_ref[...] = jnp.zeros(acc_ref.shape, f32)
    q0 = pl.multiple_of(i * tq, tq)
    r = tq // tk
    key_minus_query = (lax.broadcasted_iota(jnp.int32, (tk, tq), 0)
                       - lax.broadcasted_iota(jnp.int32, (tk, tq), 1))

    def block(j, diag):
        start = pl.multiple_of(j * tk, tk)
        for h in range(H_C):
            hs = slice(h * DH_C, (h + 1) * DH_C)
            s = jnp.dot(kb_ref[pl.ds(start, tk), hs], qt_ref[hs, :], preferred_element_type=f32)
            s = s + (ct_ref[h:h + 1, pl.ds(q0, tq)] - c_ref[pl.ds(start, tk), h:h + 1])
            if diag is not None:
                s = jnp.where(key_minus_query <= -diag * tk, s, -jnp.inf)
            m_old = m_ref[h]
            m_new = jnp.maximum(m_old, jnp.max(s, axis=0, keepdims=True))
            p = jnp.exp(s - m_new)
            alpha = jnp.exp(m_old - m_new)
            l_ref[h] = alpha * l_ref[h] + jnp.sum(p, axis=0, keepdims=True)
            acc_ref[hs, :] = alpha * acc_ref[hs, :] + jnp.dot(
                vt_ref[hs, pl.ds(start, tk)], p.astype(bf16), preferred_element_type=f32)
            m_ref[h] = m_new

    def full_block(j, carry):
        block(j, None)
        return carry

    block(i * r, 0)
    lax.fori_loop(0, i * r, full_block, 0)
    for d in range(1, r):
        block(i * r + d, d)
    for h in range(H_C):
        hs = slice(h * DH_C, (h + 1) * DH_C)
        acc_ref[hs, :] = acc_ref[hs, :] / l_ref[h]
    y_ref[...] = acc_ref[...].T.astype(y_ref.dtype)


def _fox_prompt(proj, c, ct, *, B, L, out_dtype, tq_pref=FOX_TQ, tk_pref=FOX_TK):
    tq = _pick(L, tuple(t for t in (2048, 1024, 512, 256, 128, 64, 32, 16, 8) if t <= tq_pref))
    tk = min(tq, tk_pref)
    nq = L // tq
    return pl.pallas_call(
        functools.partial(_fox_prompt_kernel, tq=tq, tk=tk),
        grid=(B, nq),
        in_specs=[pl.BlockSpec((tq, BW), lambda b, i: (b * nq + i, C_CQ)),
                  pl.BlockSpec((L, BW), lambda b, i: (b, C_CK)),
                  pl.BlockSpec((L, BW), lambda b, i: (b, C_CV)),
                  pl.BlockSpec((L, LANE), lambda b, i: (b, 0)),
                  pl.BlockSpec((None, H_C, L), lambda b, i: (b, 0, 0))],
        out_specs=pl.BlockSpec((tq, BW), lambda b, i: (b * nq + i, 0)),
        out_shape=jax.ShapeDtypeStruct((B * L, BW), out_dtype),
        scratch_shapes=[pltpu.VMEM((L, BW), bf16), pltpu.VMEM((BW, L), bf16), pltpu.VMEM((BW, tq), bf16),
                        pltpu.VMEM((H_C, 1, tq), f32), pltpu.VMEM((H_C, 1, tq), f32),
                        pltpu.VMEM((BW, tq), f32)],
        compiler_params=_cp("parallel", "arbitrary"),
        name="fox_prompt",
    )(proj, proj, proj, c, ct)


def _fox_sample_kernel(pt_ref, q_ref, kn_ref, vn_ref, cn_ref, *refs, T, npages):
    kp = refs[0:npages]
    vp = refs[npages:2 * npages]
    lfp = refs[2 * npages:3 * npages]
    y_ref = refs[3 * npages]
    P = npages * PAGE_SIZE
    lf_t = jnp.concatenate([r[...] for r in lfp], axis=1)
    lane = lax.broadcasted_iota(jnp.int32, (H_C, P), 1)
    pre = lf_t
    k = 1
    while k < P:
        pre = pre + jnp.where(lane >= k, pltpu.roll(pre, k, axis=1), 0.0)
        k *= 2
    suffix = pre[:, P - 1:P] - pre
    q = q_ref[...] * (DH_C ** -0.5)
    kn = kn_ref[...]
    vn = vn_ref[...]
    cn_t = cn_ref[...].T[0:H_C, :]
    causal_new = (lax.broadcasted_iota(jnp.int32, (T, T), 1)
                  <= lax.broadcasted_iota(jnp.int32, (T, T), 0))
    s_parts, s_new_parts = [], []
    for h in range(H_C):
        hs = slice(h * DH_C, (h + 1) * DH_C)
        qh = q[:, hs].astype(bf16)
        k_t = jnp.concatenate([r[h].astype(bf16) for r in kp], axis=1)
        s_parts.append(_dot(qh, k_t) + suffix[h:h + 1, :])
        s_new_parts.append(jnp.where(causal_new, _dot_nt(qh, kn[:, hs]) - cn_t[h:h + 1, :], -jnp.inf))
    s = jnp.concatenate(s_parts, axis=0)
    s_new = jnp.concatenate(s_new_parts, axis=0)
    m = jnp.maximum(jnp.max(s, axis=-1, keepdims=True), jnp.max(s_new, axis=-1, keepdims=True))
    p = jnp.exp(s - m)
    p_new = jnp.exp(s_new - m)
    denom = jnp.sum(p, axis=-1, keepdims=True) + jnp.sum(p_new, axis=-1, keepdims=True)
    for h in range(H_C):
        hs = slice(h * DH_C, (h + 1) * DH_C)
        rows = slice(h * T, (h + 1) * T)
        v_t = jnp.concatenate([r[h].astype(bf16) for r in vp], axis=1)
        o = _dot_nt(p[rows, :], v_t) + _dot(p_new[rows, :], vn[:, hs])
        y_ref[:, hs] = (o / denom[rows, :]).astype(y_ref.dtype)


def _fox_sample(proj, cnew, cache_kt, cache_vt, cache_lft, page_table, l, *, B, T, row_off, out_dtype):
    npages = page_table.shape[1]
    r0 = row_off // T
    kv_specs = [pl.BlockSpec((None, None, H_C, DH_C, PAGE_SIZE), lambda b, pt, p=p: (l, pt[b, p], 0, 0, 0))
                for p in range(npages)]
    lf_specs = [pl.BlockSpec((None, None, H_C, PAGE_SIZE), lambda b, pt, p=p: (l, pt[b, p], 0, 0))
                for p in range(npages)]
    grid_spec = pltpu.PrefetchScalarGridSpec(
        num_scalar_prefetch=1,
        grid=(B,),
        in_specs=[pl.BlockSpec((T, BW), lambda b, pt: (r0 + b, C_CQ)),
                  pl.BlockSpec((T, BW), lambda b, pt: (r0 + b, C_CK)),
                  pl.BlockSpec((T, BW), lambda b, pt: (r0 + b, C_CV)),
                  pl.BlockSpec((T, LANE), lambda b, pt: (b, 0))] + kv_specs + kv_specs + lf_specs,
        out_specs=pl.BlockSpec((T, BW), lambda b, pt: (b, 0)),
    )
    return pl.pallas_call(
        functools.partial(_fox_sample_kernel, T=T, npages=npages),
        grid_spec=grid_spec,
        out_shape=jax.ShapeDtypeStruct((B * T, BW), out_dtype),
        compiler_params=_cp("parallel"),
        name="fox_sample",
    )(page_table, proj, proj, proj, cnew, *([cache_kt] * npages), *([cache_vt] * npages),
      *([cache_lft] * npages))


def _xattn_kernel(q_ref, k_ref, v_ref, o_ref):
    q = q_ref[...]
    dh = q.shape[-1] // H_X
    for h in range(H_X):
        hs = slice(h * dh, (h + 1) * dh)
        s = _dot_nt(q[:, hs], k_ref[:, hs]) * (dh ** -0.5)
        m = jnp.max(s, axis=-1, keepdims=True)
        p = jnp.exp(s - m)
        p = p / jnp.sum(p, axis=-1, keepdims=True)
        o_ref[:, hs] = _dot(p, v_ref[:, hs]).astype(o_ref.dtype)


def _xattn(xq, mem_k, mem_v, lk, lv, *, B, L, row_off, out_dtype):
    D = xq.shape[1]
    n_mem = mem_k.shape[2]
    tq = _pick(L, (512, 256, 128, 64, 32, 16, 8))
    nq = L // tq
    r0 = row_off // tq
    return pl.pallas_call(
        _xattn_kernel,
        grid=(B, nq),
        in_specs=[pl.BlockSpec((tq, D), lambda b, i: (r0 + b * nq + i, 0)),
                  pl.BlockSpec((None, None, n_mem, D), lambda b, i: (lk, b, 0, 0)),
                  pl.BlockSpec((None, None, n_mem, D), lambda b, i: (lv, b, 0, 0))],
        out_specs=pl.BlockSpec((tq, D), lambda b, i: (b * nq + i, 0)),
        out_shape=jax.ShapeDtypeStruct((B * L, D), out_dtype),
        compiler_params=_cp("parallel", "arbitrary"),
        name="xattn_core",
    )(xq, mem_k, mem_v)


def _fox_kv_out_kernel(*refs, depth):
    k_refs, v_refs = refs[0:depth], refs[depth:2 * depth]
    kt_ref, vt_ref = refs[2 * depth:]
    layer = pl.program_id(0)
    for j in range(depth):
        @pl.when(layer == j)
        def _(j=j):
            kt_ref[...] = k_refs[j][...].T.reshape(kt_ref.shape)
            vt_ref[...] = v_refs[j][...].T.reshape(vt_ref.shape)


def _fox_kv_out(projs, *, B, L):
    depth = len(projs)
    tl = _pick(L, (512, 256, 128))
    nl = L // tl

    def src(j, col):
        return pl.BlockSpec((tl, BW), lambda d, b, i: (jnp.where(d == j, b * nl + i, 0), col))

    out_spec = pl.BlockSpec((None, None, H_C, DH_C, tl), lambda d, b, i: (d, b, 0, 0, i))
    out_shape = jax.ShapeDtypeStruct((depth, B, H_C, DH_C, L), f32)
    return pl.pallas_call(
        functools.partial(_fox_kv_out_kernel, depth=depth),
        grid=(depth, B, nl),
        in_specs=[src(j, C_CK) for j in range(depth)] + [src(j, C_CV) for j in range(depth)],
        out_specs=[out_spec, out_spec],
        out_shape=[out_shape, out_shape],
        compiler_params=_cp("parallel", "parallel", "parallel"),
        name="fox_kv_out",
    )(*projs, *projs)


def _xattn_decode_kernel(q_ref, k_ref, v_ref, o_ref, *, SB, T):
    n_mem, dh = k_ref.shape[1], k_ref.shape[3]
    R, NK = H_X * T, n_mem * H_X
    same_head = (lax.broadcasted_iota(jnp.int32, (R, NK), 0) // T
                 == lax.broadcasted_iota(jnp.int32, (R, NK), 1) % H_X)
    for s in range(SB):
        rs = slice(s * T, (s + 1) * T)
        q = q_ref[rs, :]
        q_stack = jnp.concatenate([q[:, h * dh:(h + 1) * dh] for h in range(H_X)], axis=0)
        kf = k_ref[s].reshape(NK, dh)
        vf = v_ref[s].reshape(NK, dh)
        sc = jnp.where(same_head, _dot_nt(q_stack, kf) * (dh ** -0.5), -jnp.inf)
        p = jnp.exp(sc - jnp.max(sc, axis=-1, keepdims=True))
        p = p / jnp.sum(p, axis=-1, keepdims=True)
        o = _dot(p, vf)
        for h in range(H_X):
            o_ref[rs, h * dh:(h + 1) * dh] = o[h * T:(h + 1) * T, :].astype(o_ref.dtype)


def _xattn_decode(xq, mem_k, mem_v, l, *, B, T, row_off, out_dtype):
    D = xq.shape[1]
    SB = _pick(B, (4, 2, 1))
    R = SB * T
    r0 = row_off // R
    kv_spec = pl.BlockSpec((None, SB) + mem_k.shape[2:], lambda b: (l, b, 0, 0, 0))
    return pl.pallas_call(
        functools.partial(_xattn_decode_kernel, SB=SB, T=T),
        grid=(B // SB,),
        in_specs=[pl.BlockSpec((R, D), lambda b: (r0 + b, 0)), kv_spec, kv_spec],
        out_specs=pl.BlockSpec((R, D), lambda b: (b, 0)),
        out_shape=jax.ShapeDtypeStruct((B * T, D), out_dtype),
        compiler_params=_cp("parallel"),
        name="xattn_decode",
    )(xq, mem_k, mem_v)


def _rope_tables(pos):
    half = DK_A // 2
    inv_freq = 1.0 / (ROPE_BASE ** (jnp.arange(half, dtype=f32) / half))
    ang = pos.astype(f32)[:, None] * inv_freq[None, :]
    cos, sin = jnp.cos(ang), jnp.sin(ang)
    return (jnp.tile(jnp.concatenate([cos, cos], axis=1), (1, H_A)),
            jnp.tile(jnp.concatenate([-sin, sin], axis=1), (1, H_A)))


def _permute_w_in(w_in):
    sizes = (256, 256, 512, 512, 512, CONV_DIM, H_B, 512, 512, 512, H_C, 256, 256, 512, 512, GLA_RANK)
    offs = [0]
    for s in sizes:
        offs.append(offs[-1] + s)
    (a_q, a_k, a_v, a_g, b_z, b_xbc, b_dt, c_q, c_k, c_v, c_f, d_q, d_k, d_v, d_r, d_lr) = (
        w_in[:, :, offs[i]:offs[i + 1]] for i in range(len(sizes)))
    d = w_in.shape[:2]
    used = 9 * 512 + 5 * 256 + H_B + H_C + GLA_RANK
    cols = [a_v, a_g, b_z, b_xbc[:, :, :BW], c_q, c_k, c_v, d_v, d_r,
            a_q, a_k, b_xbc[:, :, BW:], d_q, d_k,
            b_dt, c_f, d_lr, jnp.zeros(d + (NP_COLS - used,), w_in.dtype)]
    w_mix = jnp.concatenate(cols, axis=2).astype(bf16)
    w_gate = w_in[:, :, offs[-1]:].astype(bf16)
    return w_mix, w_gate


def _pad_lanes(x, start):
    depth, n = x.shape
    return jnp.zeros((depth, 1, LANE), f32).at[:, 0, start:start + n].set(x)


def kernel(x_prompt, x_sample, cache_fox_k, cache_fox_v, cache_fox_logf, cache_mem_k, cache_mem_v, state_ret, state_ssm, state_conv, state_gla, page_table, mem_prompt, w_in, b_gate, ret_norm_g, ssd_conv_w, ssd_conv_b, ssd_dt_bias, ssd_a_log, ssd_d, ssd_norm_g, fox_bf, gla_w_lr2, gla_b_lr, gla_norm_g, w_branch, w_out, norm_mix_g, norm_x_g, norm_mem_g, w_xq, w_mem_kv, w_xo, norm_ffn_g, w_ffn_up, w_ffn_down, norm_final_g):
    Bp, Lp, D = x_prompt.shape
    Bs, Ls, _ = x_sample.shape
    depth = w_in.shape[0]
    n_mem = mem_prompt.shape[1]
    past_len = page_table.shape[1] * PAGE_SIZE
    Tp, Ts = Bp * Lp, Bs * Ls

    w_mix, w_gate = _permute_w_in(w_in)
    w_branch_b = w_branch.astype(bf16)
    w_out_b = w_out.astype(bf16)
    w_xq_b = w_xq.astype(bf16)
    w_xo_b = w_xo.astype(bf16)
    w_kv_b = w_mem_kv.astype(bf16)
    w_up_b = w_ffn_up.astype(bf16)
    w_down_b = w_ffn_down.astype(bf16)
    b_gate4 = b_gate.reshape(depth, N_BRANCH, 1, D)
    g_mix = norm_mix_g.reshape(depth, 1, D)
    g_x = norm_x_g.reshape(depth, 1, D)
    g_mem = norm_mem_g.reshape(depth, 1, D)
    g_ffn = norm_ffn_g.reshape(depth, 1, D)
    ret_g = ret_norm_g.reshape(depth, 1, BW)
    gla_g = gla_norm_g.reshape(depth, 1, BW)
    ssd_g = ssd_norm_g.reshape(depth, 1, BW)
    conv_b = ssd_conv_b.reshape(depth, 1, CONV_DIM)
    dt_bias = _pad_lanes(ssd_dt_bias, 0)
    a_log = _pad_lanes(ssd_a_log, 0)
    d_skip = jnp.repeat(ssd_d, P_B, axis=1).reshape(depth, 1, BW)
    fox_b = _pad_lanes(fox_bf, H_B)
    w_lr2 = jnp.zeros((depth, LANE, H_D * DK_D), f32).at[:, H_B + H_C:H_B + H_C + GLA_RANK, :].set(
        gla_w_lr2).astype(bf16)
    b_lr = gla_b_lr.reshape(depth, 1, H_D * DK_D)
    cos_p, sin_p = _rope_tables(jnp.arange(Lp, dtype=jnp.int32))
    cos_s, sin_s = _rope_tables(past_len + jnp.arange(Ls, dtype=jnp.int32))
    ck_t = jnp.transpose(cache_fox_k, (0, 1, 3, 4, 2))
    cv_t = jnp.transpose(cache_fox_v, (0, 1, 3, 4, 2))
    clf_t = jnp.transpose(cache_fox_logf, (0, 1, 3, 2))
    z_ret = jnp.zeros((1, Bp, H_A, DK_A, DV_A), f32)
    z_ssm = jnp.zeros((1, Bp, H_B, N_B, P_B), f32)
    z_conv = jnp.zeros((1, Bp, CONV_W - 1, CONV_DIM), f32)
    z_gla = jnp.zeros((1, Bp, H_D, DK_D, DV_D), f32)
    mem2d = mem_prompt.reshape(Bp * n_mem, D)

    h = jnp.concatenate([x_prompt.reshape(Tp, D), x_sample.reshape(Ts, D)], axis=0)
    outs = {k: [] for k in ("ret_p", "ssm_p", "conv_p", "gla_p", "flf_p", "mk_p", "mv_p",
                            "ret_s", "ssm_s", "conv_s", "gla_s", "fk_s", "fv_s", "flf_s")}
    projs = []
    for l in range(depth):
        proj = _norm_matmul(h, g_mix, w_mix, l, tn=1024)

        pk = dict(B=Bp, L=Lp, row_off=0, out_dtype=bf16)
        sk = dict(B=Bs, L=Ls, row_off=Tp, out_dtype=f32)
        ya_p, ret_p = _retention(proj, cos_p, sin_p, z_ret, 0, ret_g, l, **pk)
        ya_s, ret_s = _retention(proj, cos_s, sin_s, state_ret, l, ret_g, l, **sk)
        yb_p, conv_p, ssm_p = _ssd(proj, z_conv, z_ssm, 0, ssd_conv_w, conv_b, dt_bias, a_log, d_skip,
                                   ssd_g, l, **pk)
        yb_s, conv_s, ssm_s = _ssd(proj, state_conv, state_ssm, l, ssd_conv_w, conv_b, dt_bias, a_log,
                                   d_skip, ssd_g, l, **sk)
        lf_p, c_p, ct_p = _fox_prep(proj, fox_b, l, B=Bp, L=Lp, row_off=0, head_major=True)
        lf_s, c_s = _fox_prep(proj, fox_b, l, B=Bs, L=Ls, row_off=Tp, head_major=False)
        yc_p = _fox_prompt(proj, c_p, ct_p, B=Bp, L=Lp, out_dtype=bf16)
        yc_s = _fox_sample(proj, c_s, ck_t, cv_t, clf_t, page_table, l, B=Bs, T=Ls, row_off=Tp, out_dtype=f32)
        yd_p, gla_p = _gla(proj, z_gla, 0, w_lr2, b_lr, gla_g, l, **pk)
        yd_s, gla_s = _gla(proj, state_gla, l, w_lr2, b_lr, gla_g, l, **sk)

        h = _merge(h, g_mix, (ya_p, yb_p, yc_p, yd_p), (ya_s, yb_s, yc_s, yd_s),
                   w_gate, b_gate4, w_branch_b, w_out_b, l)

        kv = _norm_matmul(mem2d, g_mem, w_kv_b, l, tn=D, split=True)
        mem_kv = kv.reshape(2, Bp, n_mem, D)
        xq = _norm_matmul(h, g_x, w_xq_b, l, tn=D)
        o_p = _xattn(xq, mem_kv, mem_kv, 0, 1, B=Bp, L=Lp, row_off=0, out_dtype=bf16)
        o_s = _xattn_decode(xq, cache_mem_k, cache_mem_v, l, B=Bs, T=Ls, row_off=Tp, out_dtype=f32)
        h = _xo(h, o_p, o_s, w_xo_b, l)
        h = _ffn(h, g_ffn, w_up_b, w_down_b, l)

        projs.append(proj)
        ckv = proj[Tp:, C_CK * BW:(C_CV + 1) * BW]
        outs["ret_p"].append(ret_p); outs["ssm_p"].append(ssm_p); outs["conv_p"].append(conv_p)
        outs["gla_p"].append(gla_p)
        outs["flf_p"].append(lf_p.reshape(Bp, Lp, H_C))
        outs["mk_p"].append(kv[0].reshape(Bp, n_mem, H_X, D // H_X))
        outs["mv_p"].append(kv[1].reshape(Bp, n_mem, H_X, D // H_X))
        outs["ret_s"].append(ret_s); outs["ssm_s"].append(ssm_s); outs["conv_s"].append(conv_s)
        outs["gla_s"].append(gla_s)
        outs["fk_s"].append(ckv[:, :BW].reshape(Bs, Ls, H_C, DH_C))
        outs["fv_s"].append(ckv[:, BW:].reshape(Bs, Ls, H_C, DH_C))
        outs["flf_s"].append(lf_s.reshape(Bs, Ls, H_C))

    y = _final_norm(h, norm_final_g.reshape(1, D))
    st = {k: jnp.stack(v, axis=0) for k, v in outs.items()}
    fk_t, fv_t = _fox_kv_out(projs, B=Bp, L=Lp)
    st["fk_p"] = jnp.transpose(fk_t, (0, 1, 4, 2, 3))
    st["fv_p"] = jnp.transpose(fv_t, (0, 1, 4, 2, 3))
    return (y[:Tp].reshape(Bp, Lp, D), y[Tp:].reshape(Bs, Ls, D),
            st["ret_p"], st["ssm_p"], st["conv_p"], st["gla_p"], st["fk_p"], st["fv_p"], st["flf_p"],
            st["mk_p"], st["mv_p"],
            st["ret_s"], st["ssm_s"], st["conv_s"], st["gla_s"], st["fk_s"], st["fv_s"], st["flf_s"])
```

```python
import functools
import math

import jax
import jax.numpy as jnp
from jax import lax
from jax.experimental import pallas as pl
from jax.experimental.pallas import tpu as pltpu

f32 = jnp.float32
bf16 = jnp.bfloat16

EPS = 1e-6
CHUNK = 128
GLA_CHUNK = 16
GLA_SUB = 16
CHUNKS_PER_STEP = 4
FOX_TQ, FOX_TK = 512, 512
ROPE_BASE = 10000.0
PAGE_SIZE = 128
H_A, DK_A, DV_A = 4, 64, 128
H_B, P_B, N_B, G_B, CONV_W = 8, 64, 64, 2, 4
H_C, DH_C = 8, 64
H_D, DK_D, DV_D = 4, 64, 128
GLA_RANK = 16
GLA_TAU = 16.0
N_BRANCH = 4
H_X = 4
BW = 512
CONV_DIM = BW + 2 * G_B * N_B

C_AV, C_AG, C_BZ, C_BXS, C_CQ, C_CK, C_CV, C_DV, C_DR = range(9)
C_AQ, C_AK, C_BBC, C_DQ, C_DK = range(18, 23)
C_SMALL = 46
NP_COLS = 6144
LANE = 128
VMEM_LIMIT = 56 * 1024 * 1024


def _cp(*sem):
    return pltpu.CompilerParams(dimension_semantics=sem, vmem_limit_bytes=VMEM_LIMIT)


def _pick(n, cands):
    for c in cands:
        if n % c == 0:
            return c
    raise ValueError(f"no tile for {n}")


def _dot(a, b):
    return jnp.dot(a.astype(bf16), b.astype(bf16), preferred_element_type=f32)


def _dot_nt(a, b):
    return lax.dot_general(a.astype(bf16), b.astype(bf16), (((1,), (1,)), ((), ())),
                           preferred_element_type=f32)


def _dot_tn(a, b):
    return lax.dot_general(a.astype(bf16), b.astype(bf16), (((0,), (0,)), ((), ())),
                           preferred_element_type=f32)


def _silu(x):
    return x * jax.nn.sigmoid(x)


def _log_sigmoid(x):
    return jnp.minimum(x, 0.0) - jnp.log1p(jnp.exp(-jnp.abs(x)))


def _softplus(x):
    return jnp.maximum(x, 0.0) + jnp.log1p(jnp.exp(-jnp.abs(x)))


def _cumsum_rows(x, seg=None):
    n = x.shape[0] if seg is None else seg
    row = lax.broadcasted_iota(jnp.int32, x.shape, 0)
    if n != x.shape[0]:
        row = row % n
    k = 1
    while k < n:
        x = x + jnp.where(row >= k, pltpu.roll(x, k, axis=0), 0.0)
        k *= 2
    return x


def _rms(x, g):
    return x * lax.rsqrt(jnp.mean(x * x, axis=-1, keepdims=True) + EPS) * g


def _norm_matmul_kernel(x_ref, g_ref, w_ref, o_ref, *, parts):
    xn = _rms(x_ref[...], g_ref[...]).astype(bf16)
    if parts == 1:
        o_ref[...] = jnp.dot(xn, w_ref[...], preferred_element_type=f32)
    else:
        tn = o_ref.shape[-1]
        for p in range(parts):
            o_ref[p] = jnp.dot(xn, w_ref[:, p * tn:(p + 1) * tn], preferred_element_type=f32)


def _norm_matmul(x, g, w, l, *, parts=1):
    M, K = x.shape
    N = w.shape[2]
    tn = N // parts
    rows = max(8, (12 * 1024 * 1024) // (4 * N))
    tm = _pick(M, tuple(t for t in (1024, 512, 256, 128, 64, 32, 16, 8) if t <= rows))
    if parts == 1:
        out_shape = jax.ShapeDtypeStruct((M, N), f32)
        out_spec = pl.BlockSpec((tm, N), lambda i: (i, 0))
    else:
        out_shape = jax.ShapeDtypeStruct((parts, M, tn), f32)
        out_spec = pl.BlockSpec((parts, tm, tn), lambda i: (0, i, 0))
    return pl.pallas_call(
        functools.partial(_norm_matmul_kernel, parts=parts),
        grid=(M // tm,),
        in_specs=[pl.BlockSpec((tm, K), lambda i: (i, 0)),
                  pl.BlockSpec((None, 1, K), lambda i: (l, 0, 0)),
                  pl.BlockSpec((None, K, N), lambda i: (l, 0, 0), pipeline_mode=pl.Buffered(1))],
        out_specs=out_spec,
        out_shape=out_shape,
        compiler_params=_cp("parallel"),
        name="norm_matmul",
    )(x, g, w)


def _two_src(tm, width, n_p):
    return (pl.BlockSpec((tm, width), lambda m: (jnp.minimum(m, n_p - 1), 0)),
            pl.BlockSpec((tm, width), lambda m: (jnp.maximum(m - n_p, 0), 0)))


def _merge_kernel(h_ref, g_ref, *refs, n_p):
    yp = refs[0:N_BRANCH]
    ys = refs[N_BRANCH:2 * N_BRANCH]
    wg_ref, bg_ref, wb_ref, wo_ref, o_ref = refs[2 * N_BRANCH:]
    m = pl.program_id(0)
    h = h_ref[...]
    D = h.shape[1]
    xn = _rms(h, g_ref[...]).astype(bf16)
    merged = None
    for k in range(N_BRANCH):
        gate = jax.nn.sigmoid(
            jnp.dot(xn, wg_ref[:, k * D:(k + 1) * D], preferred_element_type=f32) + bg_ref[k])
        y = jnp.where(m < n_p, yp[k][...].astype(bf16), ys[k][...].astype(bf16))
        term = gate * jnp.dot(y, wb_ref[k], preferred_element_type=f32)
        merged = term if merged is None else merged + term
    o_ref[...] = h + jnp.dot(merged.astype(bf16), wo_ref[...], preferred_element_type=f32)


def _merge(h, g, yps, yss, w_gate, b_gate, w_branch, w_out, l):
    T, D = h.shape
    Tp, Ts = yps[0].shape[0], yss[0].shape[0]
    tm = _pick(math.gcd(Tp, Ts), (512, 256, 128, 64, 32, 16, 8))
    n_p = Tp // tm
    sp, ss = _two_src(tm, BW, n_p)
    resident = lambda shape: pl.BlockSpec((None,) + shape, lambda m: (l,) + (0,) * len(shape),
                                          pipeline_mode=pl.Buffered(1))
    return pl.pallas_call(
        functools.partial(_merge_kernel, n_p=n_p),
        grid=(T // tm,),
        in_specs=[pl.BlockSpec((tm, D), lambda m: (m, 0)),
                  pl.BlockSpec((None, 1, D), lambda m: (l, 0, 0))]
                 + [sp] * N_BRANCH + [ss] * N_BRANCH
                 + [resident((D, N_BRANCH * D)), resident((N_BRANCH, 1, D)), resident((N_BRANCH, BW, D)),
                    resident((D, D))],
        out_specs=pl.BlockSpec((tm, D), lambda m: (m, 0)),
        out_shape=jax.ShapeDtypeStruct((T, D), f32),
        compiler_params=_cp("parallel"),
        name="branch_merge",
    )(h, g, *yps, *yss, w_gate, b_gate, w_branch, w_out)


def _xo_kernel(h_ref, op_ref, os_ref, w_ref, o_ref, *, n_p):
    m = pl.program_id(0)
    o = jnp.where(m < n_p, op_ref[...].astype(bf16), os_ref[...].astype(bf16))
    o_ref[...] = h_ref[...] + jnp.dot(o, w_ref[...], preferred_element_type=f32)


def _xo(h, o_p, o_s, w, l):
    T, D = h.shape
    Tp, Ts = o_p.shape[0], o_s.shape[0]
    tm = _pick(math.gcd(Tp, Ts), (1024, 512, 256, 128, 64, 32, 16, 8))
    n_p = Tp // tm
    sp, ss = _two_src(tm, D, n_p)
    return pl.pallas_call(
        functools.partial(_xo_kernel, n_p=n_p),
        grid=(T // tm,),
        in_specs=[pl.BlockSpec((tm, D), lambda m: (m, 0)), sp, ss,
                  pl.BlockSpec((None, D, D), lambda m: (l, 0, 0))],
        out_specs=pl.BlockSpec((tm, D), lambda m: (m, 0)),
        out_shape=jax.ShapeDtypeStruct((T, D), f32),
        compiler_params=_cp("parallel"),
        name="xattn_out",
    )(h, o_p, o_s, w)


def _ffn_kernel(h_ref, g_ref, wa_ref, wb_ref, wd_ref, o_ref, xn_ref, acc_ref, *, nj):
    j = pl.program_id(1)

    @pl.when(j == 0)
    def _():
        xn_ref[...] = _rms(h_ref[...], g_ref[...]).astype(bf16)
        acc_ref[...] = jnp.zeros_like(acc_ref)

    xn = xn_ref[...]
    a = jnp.dot(xn, wa_ref[...], preferred_element_type=f32)
    b = jnp.dot(xn, wb_ref[...], preferred_element_type=f32)
    acc_ref[...] += jnp.dot((_silu(a) * b).astype(bf16), wd_ref[...], preferred_element_type=f32)

    @pl.when(j == nj - 1)
    def _():
        o_ref[...] = h_ref[...] + acc_ref[...]


def _ffn(h, g, w_up, w_down, l):
    T, D = h.shape
    d_ff = w_down.shape[1]
    tm = _pick(T, (512, 256, 128, 64, 32, 16, 8))
    tf = d_ff // 2 if d_ff % (2 * LANE) == 0 else _pick(d_ff, (256, 128))
    nj = d_ff // tf
    return pl.pallas_call(
        functools.partial(_ffn_kernel, nj=nj),
        grid=(T // tm, nj),
        in_specs=[pl.BlockSpec((tm, D), lambda m, j: (m, 0)),
                  pl.BlockSpec((None, 1, D), lambda m, j: (l, 0, 0)),
                  pl.BlockSpec((None, D, tf), lambda m, j: (l, 0, j)),
                  pl.BlockSpec((None, D, tf), lambda m, j: (l, 0, j + nj)),
                  pl.BlockSpec((None, tf, D), lambda m, j: (l, j, 0))],
        out_specs=pl.BlockSpec((tm, D), lambda m, j: (m, 0)),
        out_shape=jax.ShapeDtypeStruct((T, D), f32),
        scratch_shapes=[pltpu.VMEM((tm, D), bf16), pltpu.VMEM((tm, D), f32)],
        compiler_params=_cp("parallel", "arbitrary"),
        name="swiglu_ffn",
    )(h, g, w_up, w_up, w_down)


def _final_norm_kernel(x_ref, g_ref, o_ref):
    o_ref[...] = _rms(x_ref[...], g_ref[...])


def _final_norm(h, g):
    T, D = h.shape
    tm = _pick(T, (1024, 512, 256, 128, 64, 32, 16, 8))
    return pl.pallas_call(
        _final_norm_kernel,
        grid=(T // tm,),
        in_specs=[pl.BlockSpec((tm, D), lambda m: (m, 0)), pl.BlockSpec((1, D), lambda m: (0, 0))],
        out_specs=pl.BlockSpec((tm, D), lambda m: (m, 0)),
        out_shape=jax.ShapeDtypeStruct((T, D), f32),
        compiler_params=_cp("parallel"),
        name="final_norm",
    )(h, g)


def _units_per_step(B, nchunks):
    if nchunks == 1:
        return _pick(B, (8, 4, 2, 1)), 1
    return 1, _pick(nchunks, (CHUNKS_PER_STEP, 2, 1))


def _ret_kernel(q_ref, k_ref, v_ref, g_ref, cos_ref, sin_ref, s0_ref, ng_ref, y_ref, sf_ref, S_ref,
                *, C, nch, SB, CB):
    n = pl.program_id(1)

    @pl.when(n == 0)
    def _():
        S_ref[...] = s0_ref[...]

    lane = lax.broadcasted_iota(jnp.int32, (C, H_A * DK_A), 1)
    first_half = (lane % DK_A) < (DK_A // 2)

    def rot(x, cos, sin):
        fwd = pltpu.roll(x, DK_A // 2, axis=1)
        bwd = pltpu.roll(x, H_A * DK_A - DK_A // 2, axis=1)
        return x * cos + jnp.where(first_half, bwd, fwd) * sin

    ng = ng_ref[...]
    t_col = lax.broadcasted_iota(jnp.int32, (C, 1), 0).astype(f32)
    diff = (lax.broadcasted_iota(jnp.int32, (C, C), 0)
            - lax.broadcasted_iota(jnp.int32, (C, C), 1))
    for u in range(SB * CB):
        s, cb = divmod(u, CB)
        rs = slice(u * C, (u + 1) * C)
        cos = cos_ref[cb * C:(cb + 1) * C, :]
        sin = sin_ref[cb * C:(cb + 1) * C, :]
        q = rot(q_ref[rs, :], cos, sin)
        k = rot(k_ref[rs, :], cos, sin) * (DK_A ** -0.5)
        v = v_ref[rs, :]
        gate = g_ref[rs, :]
        for h in range(H_A):
            lg = math.log1p(-2.0 ** (-5.0 - h))
            qh = q[:, h * DK_A:(h + 1) * DK_A]
            kh = k[:, h * DK_A:(h + 1) * DK_A]
            vs = slice(h * DV_A, (h + 1) * DV_A)
            vh = v[:, vs]
            decay = jnp.where(diff >= 0, jnp.exp(diff.astype(f32) * lg), 0.0)
            scores = _dot_nt(qh, kh) * decay
            s_prev = S_ref[s, h]
            o = _dot(scores, vh) + _dot(qh * jnp.exp((t_col + 1.0) * lg), s_prev)
            k_out = kh * jnp.exp((C - 1.0 - t_col) * lg)
            S_ref[s, h] = math.exp(C * lg) * s_prev + _dot_tn(k_out, vh)
            o = o - jnp.mean(o, axis=-1, keepdims=True)
            o = o * lax.rsqrt(jnp.mean(o * o, axis=-1, keepdims=True) + EPS)
            y_ref[rs, vs] = (_silu(gate[:, vs]) * (o * ng[:, vs])).astype(y_ref.dtype)

    @pl.when(n == nch - 1)
    def _():
        sf_ref[...] = S_ref[...]


def _retention(proj, cos, sin, s0, ls, ng, l, *, B, L, row_off, out_dtype):
    C = math.gcd(L, CHUNK)
    SB, CB = _units_per_step(B, L // C)
    nch = L // (C * CB)
    R = SB * CB * C
    r0 = row_off // R
    row = lambda b, n: r0 + b * nch + n
    return pl.pallas_call(
        functools.partial(_ret_kernel, C=C, nch=nch, SB=SB, CB=CB),
        grid=(B // SB, nch),
        in_specs=[pl.BlockSpec((R, 256), lambda b, n: (row(b, n), C_AQ)),
                  pl.BlockSpec((R, 256), lambda b, n: (row(b, n), C_AK)),
                  pl.BlockSpec((R, 512), lambda b, n: (row(b, n), C_AV)),
                  pl.BlockSpec((R, 512), lambda b, n: (row(b, n), C_AG)),
                  pl.BlockSpec((CB * C, 256), lambda b, n: (n, 0)),
                  pl.BlockSpec((CB * C, 256), lambda b, n: (n, 0)),
                  pl.BlockSpec((None, SB, H_A, DK_A, DV_A), lambda b, n: (ls, b, 0, 0, 0)),
                  pl.BlockSpec((None, 1, BW), lambda b, n: (l, 0, 0))],
        out_specs=[pl.BlockSpec((R, BW), lambda b, n: (b * nch + n, 0)),
                   pl.BlockSpec((SB, H_A, DK_A, DV_A), lambda b, n: (b, 0, 0, 0))],
        out_shape=[jax.ShapeDtypeStruct((B * L, BW), out_dtype),
                   jax.ShapeDtypeStruct((B, H_A, DK_A, DV_A), f32)],
        scratch_shapes=[pltpu.VMEM((SB, H_A, DK_A, DV_A), f32)],
        compiler_params=_cp("parallel", "arbitrary"),
        name="retention",
    )(proj, proj, proj, proj, cos, sin, s0, ng)


def _ssd_kernel(z_ref, xs_ref, bc_ref, sm_ref, c0_ref, s0_ref, cw_ref, cb_ref, dtb_ref, al_ref,
                dsk_ref, ng_ref, y_ref, cn_ref, sf_ref, xpad_ref, S_ref, o_ref, *, C, nch, SB, CB):
    n = pl.program_id(1)
    P0 = 8 - (CONV_W - 1)

    @pl.when(n == 0)
    def _():
        xpad_ref[:, P0:8, :] = c0_ref[...]
        S_ref[...] = s0_ref[...]

    cw = cw_ref[...]
    dsk = dsk_ref[...]
    causal = (lax.broadcasted_iota(jnp.int32, (C, C), 0)
              >= lax.broadcasted_iota(jnp.int32, (C, C), 1))
    rep = H_B // G_B
    for u in range(SB * CB):
        s = u // CB
        rs = slice(u * C, (u + 1) * C)
        xpad_ref[s, 8:8 + C, 0:BW] = xs_ref[rs, :]
        xpad_ref[s, 8:8 + C, BW:CONV_DIM] = bc_ref[rs, :]
        conv = cb_ref[...] + xpad_ref[s, P0:P0 + C, :] * cw[0:1, :]
        for j in range(1, CONV_W):
            conv = conv + xpad_ref[s, P0 + j:P0 + j + C, :] * cw[j:j + 1, :]
        tail = xpad_ref[s, C + P0:C + 8, :]
        cn_ref[s] = tail
        xpad_ref[s, P0:8, :] = tail

        act = _silu(conv)
        xs = act[:, 0:BW]
        dt = _softplus(sm_ref[rs, :] + dtb_ref[...])
        la = dt * (-jnp.exp(al_ref[...]))
        cum = _cumsum_rows(la)
        cum_t = cum.T
        cum_last = cum[C - 1:C, :]
        for g in range(G_B):
            bm = act[:, BW + g * N_B:BW + (g + 1) * N_B]
            cm = act[:, BW + G_B * N_B + g * N_B:BW + G_B * N_B + (g + 1) * N_B]
            scores_g = _dot_nt(cm, bm)
            for h in range(g * rep, (g + 1) * rep):
                ps = slice(h * P_B, (h + 1) * P_B)
                cum_h = cum[:, h:h + 1]
                seg = cum_h - cum_t[h:h + 1, :]
                decay = jnp.exp(jnp.where(causal, seg, -jnp.inf))
                xh = xs[:, ps]
                vh = xh * dt[:, h:h + 1]
                s_prev = S_ref[s, h]
                o = _dot(scores_g * decay, vh) + _dot(cm * jnp.exp(cum_h), s_prev)
                k_out = bm * jnp.exp(cum_last[:, h:h + 1] - cum_h)
                S_ref[s, h] = jnp.exp(cum_last[:, h:h + 1]) * s_prev + _dot_tn(k_out, vh)
                o_ref[u, :, ps] = o + dsk[:, ps] * xh
        y = o_ref[u] * _silu(z_ref[rs, :])
        y_ref[rs, :] = _rms(y, ng_ref[...]).astype(y_ref.dtype)

    @pl.when(n == nch - 1)
    def _():
        sf_ref[...] = S_ref[...]


def _ssd(proj, c0, s0, ls, cw, cb, dtb, al, dsk, ng, l, *, B, L, row_off, out_dtype):
    C = math.gcd(L, CHUNK)
    SB, CB = _units_per_step(B, L // C)
    nch = L // (C * CB)
    R = SB * CB * C
    r0 = row_off // R
    row = lambda b, n: r0 + b * nch + n
    par = lambda shape: pl.BlockSpec((None,) + shape, lambda b, n: (l,) + (0,) * len(shape))
    return pl.pallas_call(
        functools.partial(_ssd_kernel, C=C, nch=nch, SB=SB, CB=CB),
        grid=(B // SB, nch),
        in_specs=[pl.BlockSpec((R, 512), lambda b, n: (row(b, n), C_BZ)),
                  pl.BlockSpec((R, 512), lambda b, n: (row(b, n), C_BXS)),
                  pl.BlockSpec((R, 256), lambda b, n: (row(b, n), C_BBC)),
                  pl.BlockSpec((R, 128), lambda b, n: (row(b, n), C_SMALL)),
                  pl.BlockSpec((None, SB, CONV_W - 1, CONV_DIM), lambda b, n: (ls, b, 0, 0)),
                  pl.BlockSpec((None, SB, H_B, N_B, P_B), lambda b, n: (ls, b, 0, 0, 0)),
                  par((CONV_W, CONV_DIM)), par((1, CONV_DIM)), par((1, LANE)), par((1, LANE)),
                  par((1, BW)), par((1, BW))],
        out_specs=[pl.BlockSpec((R, BW), lambda b, n: (b * nch + n, 0)),
                   pl.BlockSpec((SB, CONV_W - 1, CONV_DIM), lambda b, n: (b, 0, 0)),
                   pl.BlockSpec((SB, H_B, N_B, P_B), lambda b, n: (b, 0, 0, 0))],
        out_shape=[jax.ShapeDtypeStruct((B * L, BW), out_dtype),
                   jax.ShapeDtypeStruct((B, CONV_W - 1, CONV_DIM), f32),
                   jax.ShapeDtypeStruct((B, H_B, N_B, P_B), f32)],
        scratch_shapes=[pltpu.VMEM((SB, C + 8, CONV_DIM), f32), pltpu.VMEM((SB, H_B, N_B, P_B), f32),
                        pltpu.VMEM((SB * CB, C, BW), f32)],
        compiler_params=_cp("parallel", "arbitrary"),
        name="ssd",
    )(proj, proj, proj, proj, c0, s0, cw, cb, dtb, al, dsk, ng)


def _gla_kernel(q_ref, k_ref, v_ref, r_ref, sm_ref, s0_ref, w2_ref, b2_ref, ng_ref, y_ref, sf_ref,
                S_ref, *, C, c, nch, SB, CB):
    n = pl.program_id(1)

    @pl.when(n == 0)
    def _():
        S_ref[...] = s0_ref[...]

    x_all = jnp.dot(sm_ref[...].astype(bf16), w2_ref[...], preferred_element_type=f32) + b2_ref[...]
    la_all = _log_sigmoid(x_all) / GLA_TAU
    for u in range(SB * CB):
        rs = slice(u * C, (u + 1) * C)
        _gla_one_chunk(la_all[rs, :], q_ref[rs, :], k_ref[rs, :], v_ref[rs, :], r_ref[rs, :], ng_ref[...],
                       y_ref, rs, S_ref, u // CB, C=C, c=c)

    @pl.when(n == nch - 1)
    def _():
        sf_ref[...] = S_ref[...]


def _gla_one_chunk(la, q, k, v, r, ng, y_ref, rs, S_ref, s, *, C, c):
    nsub = C // c
    HK = H_D * DK_D
    cum = _cumsum_rows(la)
    cum_last = cum[C - 1:C, :]
    qs = q * (DK_D ** -0.5)
    q_in = qs * jnp.exp(cum)
    k_out = k * jnp.exp(cum_last - cum)
    dec_col = jnp.broadcast_to(jnp.exp(cum_last), (8, HK)).T[:, 0:1]

    row = lax.broadcasted_iota(jnp.int32, (C, LANE), 0)
    lane = lax.broadcasted_iota(jnp.int32, (C, LANE), 1)
    causal = (lax.broadcasted_iota(jnp.int32, (C, C), 0)
              >= lax.broadcasted_iota(jnp.int32, (C, C), 1))
    for p in range(HK // LANE):
        sl = slice(p * LANE, (p + 1) * LANE)
        cum_p = cum[:, sl]
        k_p = k[:, sl]
        ref_row = jnp.zeros((C, LANE), f32)
        k_pieces = []
        for i in range(nsub):
            ref_i = cum_p[i * c - 1:i * c, :] if i > 0 else jnp.zeros((1, LANE), f32)
            in_blk = (row >= i * c) & (row < (i + 1) * c)
            ref_row = jnp.where(in_blk, ref_i, ref_row)
            expo = jnp.where(row < (i + 1) * c, ref_i - cum_p, -jnp.inf)
            k_pieces.append((k_p * jnp.exp(expo)).astype(bf16))
        k_big = jnp.concatenate(k_pieces, axis=1) if nsub > 1 else k_pieces[0]
        q_sc = qs[:, sl] * jnp.exp(cum_p - ref_row)
        for hh in range(LANE // DK_D):
            h = p * (LANE // DK_D) + hh
            in_head = (lane >= hh * DK_D) & (lane < (hh + 1) * DK_D)
            q_pieces = []
            for i in range(nsub):
                in_blk = (row >= i * c) & (row < (i + 1) * c)
                q_pieces.append(jnp.where(in_blk & in_head, q_sc, 0.0).astype(bf16))
            q_big = jnp.concatenate(q_pieces, axis=1) if nsub > 1 else q_pieces[0]
            scores = jnp.where(causal, _dot_nt(q_big, k_big), 0.0)
            vs = slice(h * DV_D, (h + 1) * DV_D)
            vh = v[:, vs]
            s_prev = S_ref[s, h]
            hs = slice(h * DK_D, (h + 1) * DK_D)
            o = _dot(scores, vh) + _dot(q_in[:, hs], s_prev)
            S_ref[s, h] = dec_col[hs, :] * s_prev + _dot_tn(k_out[:, hs], vh)
            o = o * lax.rsqrt(jnp.mean(o * o, axis=-1, keepdims=True) + EPS)
            y_ref[rs, vs] = (_silu(r[:, vs]) * (o * ng[:, vs])).astype(y_ref.dtype)


def _gla(proj, s0, ls, w2, b2, ng, l, *, B, L, row_off, out_dtype):
    C = math.gcd(L, CHUNK)
    c = math.gcd(L, GLA_SUB)
    SB, CB = _units_per_step(B, L // C)
    nch = L // (C * CB)
    R = SB * CB * C
    r0 = row_off // R
    row = lambda b, n: r0 + b * nch + n
    par = lambda shape: pl.BlockSpec((None,) + shape, lambda b, n: (l,) + (0,) * len(shape))
    return pl.pallas_call(
        functools.partial(_gla_kernel, C=C, c=c, nch=nch, SB=SB, CB=CB),
        grid=(B // SB, nch),
        in_specs=[pl.BlockSpec((R, 256), lambda b, n: (row(b, n), C_DQ)),
                  pl.BlockSpec((R, 256), lambda b, n: (row(b, n), C_DK)),
                  pl.BlockSpec((R, 512), lambda b, n: (row(b, n), C_DV)),
                  pl.BlockSpec((R, 512), lambda b, n: (row(b, n), C_DR)),
                  pl.BlockSpec((R, 128), lambda b, n: (row(b, n), C_SMALL)),
                  pl.BlockSpec((None, SB, H_D, DK_D, DV_D), lambda b, n: (ls, b, 0, 0, 0)),
                  par((LANE, H_D * DK_D)), par((1, H_D * DK_D)), par((1, BW))],
        out_specs=[pl.BlockSpec((R, BW), lambda b, n: (b * nch + n, 0)),
                   pl.BlockSpec((SB, H_D, DK_D, DV_D), lambda b, n: (b, 0, 0, 0))],
        out_shape=[jax.ShapeDtypeStruct((B * L, BW), out_dtype),
                   jax.ShapeDtypeStruct((B, H_D, DK_D, DV_D), f32)],
        scratch_shapes=[pltpu.VMEM((SB, H_D, DK_D, DV_D), f32)],
        compiler_params=_cp("parallel", "arbitrary"),
        name="gla",
    )(proj, proj, proj, proj, proj, s0, w2, b2, ng)


def _fox_prep_kernel(sm_ref, bf_ref, lf_ref, c_ref, *carry, C):
    chained = bool(carry)
    lf = _log_sigmoid(sm_ref[...] + bf_ref[...])
    lf = pltpu.roll(lf, LANE - H_C, axis=1)
    lane = lax.broadcasted_iota(jnp.int32, lf.shape, 1)
    lf = jnp.where(lane < H_C, lf, 0.0)
    cs = _cumsum_rows(lf, seg=C)
    if chained:
        carry_ref, = carry

        @pl.when(pl.program_id(1) == 0)
        def _():
            carry_ref[...] = jnp.zeros_like(carry_ref)

        cs = carry_ref[...] + cs
        carry_ref[...] = cs[C - 1:C, :]
    lf_ref[...] = lf[:, 0:H_C]
    c_ref[...] = cs


def _fox_prep(proj, bf, l, *, B, L, row_off):
    C = math.gcd(L, 4 * CHUNK)
    nch = L // C
    chained = nch > 1
    SB = 1 if chained else _pick(B, (16, 8, 4, 2, 1))
    R = SB * C
    r0 = row_off // R
    return pl.pallas_call(
        functools.partial(_fox_prep_kernel, C=C),
        grid=(B // SB, nch),
        in_specs=[pl.BlockSpec((R, LANE), lambda b, n: (r0 + b * nch + n, C_SMALL)),
                  pl.BlockSpec((None, 1, LANE), lambda b, n: (l, 0, 0))],
        out_specs=[pl.BlockSpec((R, H_C), lambda b, n: (b * nch + n, 0)),
                   pl.BlockSpec((R, LANE), lambda b, n: (b * nch + n, 0))],
        out_shape=[jax.ShapeDtypeStruct((B * L, H_C), f32), jax.ShapeDtypeStruct((B * L, LANE), f32)],
        scratch_shapes=[pltpu.VMEM((1, LANE), f32)] if chained else [],
        compiler_params=_cp("parallel", "arbitrary"),
        name="fox_prep",
    )(proj, bf)


def _fox_prompt_kernel(q_ref, k_ref, v_ref, c_ref, y_ref, kb_ref, vt_ref, qt_ref, m_ref, l_ref,
                       acc_ref, *, tq, tk):
    i = pl.program_id(1)

    @pl.when(i == 0)
    def _():
        kb_ref[...] = k_ref[...].astype(bf16)
        vt_ref[...] = v_ref[...].T.astype(bf16)

    qt_ref[...] = (q_ref[...] * (DH_C ** -0.5)).T.astype(bf16)
    m_ref[...] = jnp.full(m_ref.shape, -jnp.inf, f32)
    l_ref[...] = jnp.zeros(l_ref.shape, f32)
    acc_ref[...] = jnp.zeros(acc_ref.shape, f32)
    r = tq // tk
    key_minus_query = (lax.broadcasted_iota(jnp.int32, (tk, tq), 0)
                       - lax.broadcasted_iota(jnp.int32, (tk, tq), 1))

    def block(j, diag):
        start = pl.multiple_of(j * tk, tk)
        for h in range(H_C):
            hs = slice(h * DH_C, (h + 1) * DH_C)
            s = jnp.dot(kb_ref[pl.ds(start, tk), hs], qt_ref[hs, :], preferred_element_type=f32)
            s = s - c_ref[pl.ds(start, tk), h:h + 1]
            if diag is not None:
                s = jnp.where(key_minus_query <= -diag * tk, s, -jnp.inf)
            m_old = m_ref[h]
            m_new = jnp.maximum(m_old, jnp.max(s, axis=0, keepdims=True))
            p = jnp.exp(s - m_new)
            alpha = jnp.exp(m_old - m_new)
            l_ref[h] = alpha * l_ref[h] + jnp.sum(p, axis=0, keepdims=True)
            acc_ref[hs, :] = alpha * acc_ref[hs, :] + jnp.dot(
                vt_ref[hs, pl.ds(start, tk)], p.astype(bf16), preferred_element_type=f32)
            m_ref[h] = m_new

    def full_block(j, carry):
        block(j, None)
        return carry

    block(i * r, 0)
    lax.fori_loop(0, i * r, full_block, 0)
    for d in range(1, r):
        block(i * r + d, d)
    for h in range(H_C):
        hs = slice(h * DH_C, (h + 1) * DH_C)
        acc_ref[hs, :] = acc_ref[hs, :] / l_ref[h]
    y_ref[...] = acc_ref[...].T.astype(y_ref.dtype)


def _fox_prompt(proj, c, *, B, L, out_dtype, tq_pref=FOX_TQ, tk_pref=FOX_TK):
    tq = _pick(L, tuple(t for t in (2048, 1024, 512, 256, 128, 64, 32, 16, 8) if t <= tq_pref))
    tk = min(tq, tk_pref)
    nq = L // tq
    return pl.pallas_call(
        functools.partial(_fox_prompt_kernel, tq=tq, tk=tk),
        grid=(B, nq),
        in_specs=[pl.BlockSpec((tq, BW), lambda b, i: (b * nq + i, C_CQ)),
                  pl.BlockSpec((L, BW), lambda b, i: (b, C_CK)),
                  pl.BlockSpec((L, BW), lambda b, i: (b, C_CV)),
                  pl.BlockSpec((L, LANE), lambda b, i: (b, 0))],
        out_specs=pl.BlockSpec((tq, BW), lambda b, i: (b * nq + i, 0)),
        out_shape=jax.ShapeDtypeStruct((B * L, BW), out_dtype),
        scratch_shapes=[pltpu.VMEM((L, BW), bf16), pltpu.VMEM((BW, L), bf16), pltpu.VMEM((BW, tq), bf16),
                        pltpu.VMEM((H_C, 1, tq), f32), pltpu.VMEM((H_C, 1, tq), f32),
                        pltpu.VMEM((BW, tq), f32)],
        compiler_params=_cp("parallel", "arbitrary"),
        name="fox_prompt",
    )(proj, proj, proj, c)


def _fox_sample_kernel(pt_ref, q_ref, kn_ref, vn_ref, cn_ref, *refs, T, npages):
    kp = refs[0:npages]
    vp = refs[npages:2 * npages]
    lfp = refs[2 * npages:3 * npages]
    y_ref = refs[3 * npages]
    P = npages * PAGE_SIZE
    lf_t = jnp.concatenate([r[...] for r in lfp], axis=1)
    lane = lax.broadcasted_iota(jnp.int32, (H_C, P), 1)
    pre = lf_t
    k = 1
    while k < P:
        pre = pre + jnp.where(lane >= k, pltpu.roll(pre, k, axis=1), 0.0)
        k *= 2
    suffix = pre[:, P - 1:P] - pre
    q = q_ref[...] * (DH_C ** -0.5)
    kn = kn_ref[...]
    vn = vn_ref[...]
    cn_t = cn_ref[...].T[0:H_C, :]
    causal_new = (lax.broadcasted_iota(jnp.int32, (T, T), 1)
                  <= lax.broadcasted_iota(jnp.int32, (T, T), 0))
    s_parts, s_new_parts = [], []
    for h in range(H_C):
        hs = slice(h * DH_C, (h + 1) * DH_C)
        qh = q[:, hs].astype(bf16)
        k_t = jnp.concatenate([r[h].astype(bf16) for r in kp], axis=1)
        s_parts.append(_dot(qh, k_t) + suffix[h:h + 1, :])
        s_new_parts.append(jnp.where(causal_new, _dot_nt(qh, kn[:, hs]) - cn_t[h:h + 1, :], -jnp.inf))
    s = jnp.concatenate(s_parts, axis=0)
    s_new = jnp.concatenate(s_new_parts, axis=0)
    m = jnp.maximum(jnp.max(s, axis=-1, keepdims=True), jnp.max(s_new, axis=-1, keepdims=True))
    p = jnp.exp(s - m)
    p_new = jnp.exp(s_new - m)
    denom = jnp.sum(p, axis=-1, keepdims=True) + jnp.sum(p_new, axis=-1, keepdims=True)
    for h in range(H_C):
        hs = slice(h * DH_C, (h + 1) * DH_C)
        rows = slice(h * T, (h + 1) * T)
        v_t = jnp.concatenate([r[h].astype(bf16) for r in vp], axis=1)
        o = _dot_nt(p[rows, :], v_t) + _dot(p_new[rows, :], vn[:, hs])
        y_ref[:, hs] = (o / denom[rows, :]).astype(y_ref.dtype)


def _fox_sample(proj, cnew, cache_kt, cache_vt, cache_lft, page_table, l, *, B, T, row_off, out_dtype):
    npages = page_table.shape[1]
    r0 = row_off // T
    kv_specs = [pl.BlockSpec((None, None, H_C, DH_C, PAGE_SIZE), lambda b, pt, p=p: (l, pt[b, p], 0, 0, 0))
                for p in range(npages)]
    lf_specs = [pl.BlockSpec((None, None, H_C, PAGE_SIZE), lambda b, pt, p=p: (l, pt[b, p], 0, 0))
                for p in range(npages)]
    grid_spec = pltpu.PrefetchScalarGridSpec(
        num_scalar_prefetch=1,
        grid=(B,),
        in_specs=[pl.BlockSpec((T, BW), lambda b, pt: (r0 + b, C_CQ)),
                  pl.BlockSpec((T, BW), lambda b, pt: (r0 + b, C_CK)),
                  pl.BlockSpec((T, BW), lambda b, pt: (r0 + b, C_CV)),
                  pl.BlockSpec((T, LANE), lambda b, pt: (b, 0))] + kv_specs + kv_specs + lf_specs,
        out_specs=pl.BlockSpec((T, BW), lambda b, pt: (b, 0)),
    )
    return pl.pallas_call(
        functools.partial(_fox_sample_kernel, T=T, npages=npages),
        grid_spec=grid_spec,
        out_shape=jax.ShapeDtypeStruct((B * T, BW), out_dtype),
        compiler_params=_cp("parallel"),
        name="fox_sample",
    )(page_table, proj, proj, proj, cnew, *([cache_kt] * npages), *([cache_vt] * npages),
      *([cache_lft] * npages))


def _xattn_kernel(q_ref, k_ref, v_ref, o_ref):
    q = q_ref[...]
    dh = q.shape[-1] // H_X
    for h in range(H_X):
        hs = slice(h * dh, (h + 1) * dh)
        s = _dot_nt(q[:, hs], k_ref[:, hs]) * (dh ** -0.5)
        m = jnp.max(s, axis=-1, keepdims=True)
        p = jnp.exp(s - m)
        p = p / jnp.sum(p, axis=-1, keepdims=True)
        o_ref[:, hs] = _dot(p, v_ref[:, hs]).astype(o_ref.dtype)


def _xattn(xq, mem_k, mem_v, lk, lv, *, B, L, row_off, out_dtype):
    D = xq.shape[1]
    n_mem = mem_k.shape[2]
    tq = _pick(L, (512, 256, 128, 64, 32, 16, 8))
    nq = L // tq
    r0 = row_off // tq
    return pl.pallas_call(
        _xattn_kernel,
        grid=(B, nq),
        in_specs=[pl.BlockSpec((tq, D), lambda b, i: (r0 + b * nq + i, 0)),
                  pl.BlockSpec((None, None, n_mem, D), lambda b, i: (lk, b, 0, 0)),
                  pl.BlockSpec((None, None, n_mem, D), lambda b, i: (lv, b, 0, 0))],
        out_specs=pl.BlockSpec((tq, D), lambda b, i: (b * nq + i, 0)),
        out_shape=jax.ShapeDtypeStruct((B * L, D), out_dtype),
        compiler_params=_cp("parallel", "arbitrary"),
        name="xattn_core",
    )(xq, mem_k, mem_v)


def _fox_kv_out_kernel(*refs, depth):
    k_refs, v_refs = refs[0:depth], refs[depth:2 * depth]
    kt_ref, vt_ref = refs[2 * depth:]
    layer = pl.program_id(0)
    for j in range(depth):
        @pl.when(layer == j)
        def _(j=j):
            kt_ref[...] = k_refs[j][...].T.reshape(kt_ref.shape)
            vt_ref[...] = v_refs[j][...].T.reshape(vt_ref.shape)


def _fox_kv_out(projs, *, B, L):
    depth = len(projs)
    tl = _pick(L, (512, 256, 128))
    nl = L // tl

    def src(j, col):
        return pl.BlockSpec((tl, BW), lambda d, b, i: (jnp.where(d == j, b * nl + i, 0), col))

    out_spec = pl.BlockSpec((None, None, H_C, DH_C, tl), lambda d, b, i: (d, b, 0, 0, i))
    out_shape = jax.ShapeDtypeStruct((depth, B, H_C, DH_C, L), f32)
    return pl.pallas_call(
        functools.partial(_fox_kv_out_kernel, depth=depth),
        grid=(depth, B, nl),
        in_specs=[src(j, C_CK) for j in range(depth)] + [src(j, C_CV) for j in range(depth)],
        out_specs=[out_spec, out_spec],
        out_shape=[out_shape, out_shape],
        compiler_params=_cp("parallel", "parallel", "parallel"),
        name="fox_kv_out",
    )(*projs, *projs)


def _xattn_decode_kernel(q_ref, k_ref, v_ref, o_ref, *, SB, T):
    n_mem, dh = k_ref.shape[1], k_ref.shape[3]
    R, NK = H_X * T, n_mem * H_X
    same_head = (lax.broadcasted_iota(jnp.int32, (R, NK), 0) // T
                 == lax.broadcasted_iota(jnp.int32, (R, NK), 1) % H_X)
    for s in range(SB):
        rs = slice(s * T, (s + 1) * T)
        q = q_ref[rs, :]
        q_stack = jnp.concatenate([q[:, h * dh:(h + 1) * dh] for h in range(H_X)], axis=0)
        kf = k_ref[s].reshape(NK, dh)
        vf = v_ref[s].reshape(NK, dh)
        sc = jnp.where(same_head, _dot_nt(q_stack, kf) * (dh ** -0.5), -jnp.inf)
        p = jnp.exp(sc - jnp.max(sc, axis=-1, keepdims=True))
        p = p / jnp.sum(p, axis=-1, keepdims=True)
        o = _dot(p, vf)
        for h in range(H_X):
            o_ref[rs, h * dh:(h + 1) * dh] = o[h * T:(h + 1) * T, :].astype(o_ref.dtype)


def _xattn_decode(xq, mem_k, mem_v, l, *, B, T, row_off, out_dtype):
    D = xq.shape[1]
    SB = _pick(B, (4, 2, 1))
    R = SB * T
    r0 = row_off // R
    kv_spec = pl.BlockSpec((None, SB) + mem_k.shape[2:], lambda b: (l, b, 0, 0, 0))
    return pl.pallas_call(
        functools.partial(_xattn_decode_kernel, SB=SB, T=T),
        grid=(B // SB,),
        in_specs=[pl.BlockSpec((R, D), lambda b: (r0 + b, 0)), kv_spec, kv_spec],
        out_specs=pl.BlockSpec((R, D), lambda b: (b, 0)),
        out_shape=jax.ShapeDtypeStruct((B * T, D), out_dtype),
        compiler_params=_cp("parallel"),
        name="xattn_decode",
    )(xq, mem_k, mem_v)


def _rope_tables(pos):
    half = DK_A // 2
    inv_freq = 1.0 / (ROPE_BASE ** (jnp.arange(half, dtype=f32) / half))
    ang = pos.astype(f32)[:, None] * inv_freq[None, :]
    cos, sin = jnp.cos(ang), jnp.sin(ang)
    return (jnp.tile(jnp.concatenate([cos, cos], axis=1), (1, H_A)),
            jnp.tile(jnp.concatenate([-sin, sin], axis=1), (1, H_A)))


def _w_in_segments():
    names = ("a_q", "a_k", "a_v", "a_g", "b_z", "b_xs", "b_bc", "b_dt", "c_q", "c_k", "c_v", "c_f",
             "d_q", "d_k", "d_v", "d_r", "d_lr")
    sizes = (256, 256, 512, 512, 512, BW, CONV_DIM - BW, H_B, 512, 512, 512, H_C, 256, 256, 512, 512, GLA_RANK)
    src, off = {}, 0
    for n, s in zip(names, sizes):
        src[n] = (off, s)
        off += s
    wide = [src[n] for n in ("a_v", "a_g", "b_z", "b_xs", "c_q", "c_k", "c_v", "d_v", "d_r",
                             "a_q", "a_k", "b_bc", "d_q", "d_k")]
    narrow = [src[n] for n in ("b_dt", "c_f", "d_lr")]
    return wide, narrow, off


def _w_in_prep_kernel(w_ref, mix_ref, gate_ref):
    wide, narrow, gate_start = _w_in_segments()
    dst = 0
    for start, width in wide:
        mix_ref[:, dst:dst + width] = w_ref[:, start:start + width].astype(bf16)
        dst += width
    rows = w_ref.shape[0]
    lane = lax.broadcasted_iota(jnp.int32, (rows, LANE), 1)
    small = jnp.zeros((rows, LANE), f32)
    for start, width in narrow:
        base = start - start % LANE
        in_seg = (lane >= start - base) & (lane < start - base + width)
        small = jnp.where(in_seg, w_ref[:, base:base + LANE], small)
    mix_ref[:, dst:dst + LANE] = small.astype(bf16)
    mix_ref[:, dst + LANE:] = jnp.zeros((rows, mix_ref.shape[1] - dst - LANE), bf16)
    gate_ref[...] = w_ref[:, gate_start:].astype(bf16)


def _permute_w_in(w_in):
    depth, D, n_in = w_in.shape
    n_gate = N_BRANCH * D
    _, narrow, gate_start = _w_in_segments()
    assert [s % LANE for s, _ in narrow] == [0, H_B, H_B + H_C] and gate_start + n_gate == n_in
    tk = _pick(D, (256, 128, 64, 32, 16))
    return pl.pallas_call(
        _w_in_prep_kernel,
        grid=(depth, D // tk),
        in_specs=[pl.BlockSpec((None, tk, n_in), lambda d, i: (d, i, 0))],
        out_specs=[pl.BlockSpec((None, tk, NP_COLS), lambda d, i: (d, i, 0)),
                   pl.BlockSpec((None, tk, n_gate), lambda d, i: (d, i, 0))],
        out_shape=[jax.ShapeDtypeStruct((depth, D, NP_COLS), bf16),
                   jax.ShapeDtypeStruct((depth, D, n_gate), bf16)],
        compiler_params=_cp("parallel", "parallel"),
        name="w_in_layout",
    )(w_in)


def _pad_lanes(x, start):
    depth, n = x.shape
    return jnp.zeros((depth, 1, LANE), f32).at[:, 0, start:start + n].set(x)


def kernel(x_prompt, x_sample, cache_fox_k, cache_fox_v, cache_fox_logf, cache_mem_k, cache_mem_v, state_ret, state_ssm, state_conv, state_gla, page_table, mem_prompt, w_in, b_gate, ret_norm_g, ssd_conv_w, ssd_conv_b, ssd_dt_bias, ssd_a_log, ssd_d, ssd_norm_g, fox_bf, gla_w_lr2, gla_b_lr, gla_norm_g, w_branch, w_out, norm_mix_g, norm_x_g, norm_mem_g, w_xq, w_mem_kv, w_xo, norm_ffn_g, w_ffn_up, w_ffn_down, norm_final_g):
    Bp, Lp, D = x_prompt.shape
    Bs, Ls, _ = x_sample.shape
    depth = w_in.shape[0]
    n_mem = mem_prompt.shape[1]
    past_len = page_table.shape[1] * PAGE_SIZE
    Tp, Ts = Bp * Lp, Bs * Ls

    w_mix, w_gate = _permute_w_in(w_in)
    w_branch_b = w_branch.astype(bf16)
    w_out_b = w_out.astype(bf16)
    w_xq_b = w_xq.astype(bf16)
    w_xo_b = w_xo.astype(bf16)
    w_kv_b = w_mem_kv.astype(bf16)
    w_up_b = w_ffn_up.astype(bf16)
    w_down_b = w_ffn_down.astype(bf16)
    b_gate4 = b_gate.reshape(depth, N_BRANCH, 1, D)
    g_mix = norm_mix_g.reshape(depth, 1, D)
    g_x = norm_x_g.reshape(depth, 1, D)
    g_mem = norm_mem_g.reshape(depth, 1, D)
    g_ffn = norm_ffn_g.reshape(depth, 1, D)
    ret_g = ret_norm_g.reshape(depth, 1, BW)
    gla_g = gla_norm_g.reshape(depth, 1, BW)
    ssd_g = ssd_norm_g.reshape(depth, 1, BW)
    conv_b = ssd_conv_b.reshape(depth, 1, CONV_DIM)
    dt_bias = _pad_lanes(ssd_dt_bias, 0)
    a_log = _pad_lanes(ssd_a_log, 0)
    d_skip = jnp.repeat(ssd_d, P_B, axis=1).reshape(depth, 1, BW)
    fox_b = _pad_lanes(fox_bf, H_B)
    w_lr2 = jnp.zeros((depth, LANE, H_D * DK_D), f32).at[:, H_B + H_C:H_B + H_C + GLA_RANK, :].set(
        gla_w_lr2).astype(bf16)
    b_lr = gla_b_lr.reshape(depth, 1, H_D * DK_D)
    cos_p, sin_p = _rope_tables(jnp.arange(Lp, dtype=jnp.int32))
    cos_s, sin_s = _rope_tables(past_len + jnp.arange(Ls, dtype=jnp.int32))
    ck_t = jnp.transpose(cache_fox_k, (0, 1, 3, 4, 2))
    cv_t = jnp.transpose(cache_fox_v, (0, 1, 3, 4, 2))
    clf_t = jnp.transpose(cache_fox_logf, (0, 1, 3, 2))
    z_ret = jnp.zeros((1, Bp, H_A, DK_A, DV_A), f32)
    z_ssm = jnp.zeros((1, Bp, H_B, N_B, P_B), f32)
    z_conv = jnp.zeros((1, Bp, CONV_W - 1, CONV_DIM), f32)
    z_gla = jnp.zeros((1, Bp, H_D, DK_D, DV_D), f32)
    mem2d = mem_prompt.reshape(Bp * n_mem, D)

    h = jnp.concatenate([x_prompt.reshape(Tp, D), x_sample.reshape(Ts, D)], axis=0)
    outs = {k: [] for k in ("ret_p", "ssm_p", "conv_p", "gla_p", "flf_p", "mk_p", "mv_p",
                            "ret_s", "ssm_s", "conv_s", "gla_s", "fk_s", "fv_s", "flf_s")}
    projs = []
    for l in range(depth):
        proj = _norm_matmul(h, g_mix, w_mix, l)

        pk = dict(B=Bp, L=Lp, row_off=0, out_dtype=bf16)
        sk = dict(B=Bs, L=Ls, row_off=Tp, out_dtype=f32)
        ya_p, ret_p = _retention(proj, cos_p, sin_p, z_ret, 0, ret_g, l, **pk)
        ya_s, ret_s = _retention(proj, cos_s, sin_s, state_ret, l, ret_g, l, **sk)
        yb_p, conv_p, ssm_p = _ssd(proj, z_conv, z_ssm, 0, ssd_conv_w, conv_b, dt_bias, a_log, d_skip,
                                   ssd_g, l, **pk)
        yb_s, conv_s, ssm_s = _ssd(proj, state_conv, state_ssm, l, ssd_conv_w, conv_b, dt_bias, a_log,
                                   d_skip, ssd_g, l, **sk)
        lf_p, c_p = _fox_prep(proj, fox_b, l, B=Bp, L=Lp, row_off=0)
        lf_s, c_s = _fox_prep(proj, fox_b, l, B=Bs, L=Ls, row_off=Tp)
        yc_p = _fox_prompt(proj, c_p, B=Bp, L=Lp, out_dtype=bf16)
        yc_s = _fox_sample(proj, c_s, ck_t, cv_t, clf_t, page_table, l, B=Bs, T=Ls, row_off=Tp, out_dtype=f32)
        yd_p, gla_p = _gla(proj, z_gla, 0, w_lr2, b_lr, gla_g, l, **pk)
        yd_s, gla_s = _gla(proj, state_gla, l, w_lr2, b_lr, gla_g, l, **sk)

        h = _merge(h, g_mix, (ya_p, yb_p, yc_p, yd_p), (ya_s, yb_s, yc_s, yd_s),
                   w_gate, b_gate4, w_branch_b, w_out_b, l)

        kv = _norm_matmul(mem2d, g_mem, w_kv_b, l, parts=2)
        mem_kv = kv.reshape(2, Bp, n_mem, D)
        xq = _norm_matmul(h, g_x, w_xq_b, l)
        o_p = _xattn(xq, mem_kv, mem_kv, 0, 1, B=Bp, L=Lp, row_off=0, out_dtype=bf16)
        o_s = _xattn_decode(xq, cache_mem_k, cache_mem_v, l, B=Bs, T=Ls, row_off=Tp, out_dtype=f32)
        h = _xo(h, o_p, o_s, w_xo_b, l)
        h = _ffn(h, g_ffn, w_up_b, w_down_b, l)

        projs.append(proj)
        ckv = proj[Tp:, C_CK * BW:(C_CV + 1) * BW]
        outs["ret_p"].append(ret_p); outs["ssm_p"].append(ssm_p); outs["conv_p"].append(conv_p)
        outs["gla_p"].append(gla_p)
        outs["flf_p"].append(lf_p.reshape(Bp, Lp, H_C))
        outs["mk_p"].append(kv[0].reshape(Bp, n_mem, H_X, D // H_X))
        outs["mv_p"].append(kv[1].reshape(Bp, n_mem, H_X, D // H_X))
        outs["ret_s"].append(ret_s); outs["ssm_s"].append(ssm_s); outs["conv_s"].append(conv_s)
        outs["gla_s"].append(gla_s)
        outs["fk_s"].append(ckv[:, :BW].reshape(Bs, Ls, H_C, DH_C))
        outs["fv_s"].append(ckv[:, BW:].reshape(Bs, Ls, H_C, DH_C))
        outs["flf_s"].append(lf_s.reshape(Bs, Ls, H_C))

    y = _final_norm(h, norm_final_g.reshape(1, D))
    st = {k: jnp.stack(v, axis=0) for k, v in outs.items()}
    fk_t, fv_t = _fox_kv_out(projs, B=Bp, L=Lp)
    st["fk_p"] = jnp.transpose(fk_t, (0, 1, 4, 2, 3))
    st["fv_p"] = jnp.transpose(fv_t, (0, 1, 4, 2, 3))
    return (y[:Tp].reshape(Bp, Lp, D), y[Tp:].reshape(Bs, Ls, D),
            st["ret_p"], st["ssm_p"], st["conv_p"], st["gla_p"], st["fk_p"], st["fv_p"], st["flf_p"],
            st["mk_p"], st["mv_p"],
            st["ret_s"], st["ssm_s"], st["conv_s"], st["gla_s"], st["fk_s"], st["fv_s"], st["flf_s"])
```

```python
import functools
import math

import jax
import jax.numpy as jnp
from jax import lax
from jax.experimental import pallas as pl
from jax.experimental.pallas import tpu as pltpu

f32 = jnp.float32
bf16 = jnp.bfloat16

EPS = 1e-6
CHUNK = 128
GLA_CHUNK = 16
GLA_SUB = 16
CHUNKS_PER_STEP = 4
FOX_TQ, FOX_TK = 512, 512
ROPE_BASE = 10000.0
PAGE_SIZE = 128
H_A, DK_A, DV_A = 4, 64, 128
H_B, P_B, N_B, G_B, CONV_W = 8, 64, 64, 2, 4
H_C, DH_C = 8, 64
H_D, DK_D, DV_D = 4, 64, 128
GLA_RANK = 16
GLA_TAU = 16.0
N_BRANCH = 4
H_X = 4
BW = 512
CONV_DIM = BW + 2 * G_B * N_B

C_AV, C_AG, C_BZ, C_BXS, C_CQ, C_CK, C_CV, C_DV, C_DR = range(9)
C_AQ, C_AK, C_BBC, C_DQ, C_DK = range(18, 23)
C_SMALL = 46
NP_COLS = 6144
LANE = 128
VMEM_LIMIT = 56 * 1024 * 1024


def _cp(*sem):
    return pltpu.CompilerParams(dimension_semantics=sem, vmem_limit_bytes=VMEM_LIMIT)


def _pick(n, cands):
    for c in cands:
        if n % c == 0:
            return c
    raise ValueError(f"no tile for {n}")


def _dot(a, b):
    return jnp.dot(a.astype(bf16), b.astype(bf16), preferred_element_type=f32)


def _dot_nt(a, b):
    return lax.dot_general(a.astype(bf16), b.astype(bf16), (((1,), (1,)), ((), ())),
                           preferred_element_type=f32)


def _dot_tn(a, b):
    return lax.dot_general(a.astype(bf16), b.astype(bf16), (((0,), (0,)), ((), ())),
                           preferred_element_type=f32)


def _silu(x):
    return x * jax.nn.sigmoid(x)


def _log_sigmoid(x):
    return jnp.minimum(x, 0.0) - jnp.log1p(jnp.exp(-jnp.abs(x)))


def _softplus(x):
    return jnp.maximum(x, 0.0) + jnp.log1p(jnp.exp(-jnp.abs(x)))


def _cumsum_rows(x, seg=None):
    n = x.shape[0] if seg is None else seg
    row = lax.broadcasted_iota(jnp.int32, x.shape, 0)
    if n != x.shape[0]:
        row = row % n
    k = 1
    while k < n:
        x = x + jnp.where(row >= k, pltpu.roll(x, k, axis=0), 0.0)
        k *= 2
    return x


def _rms(x, g):
    return x * lax.rsqrt(jnp.mean(x * x, axis=-1, keepdims=True) + EPS) * g


def _norm_matmul_kernel(x_ref, g_ref, w_ref, o_ref, *, parts):
    xn = _rms(x_ref[...], g_ref[...]).astype(bf16)
    if parts == 1:
        o_ref[...] = jnp.dot(xn, w_ref[...], preferred_element_type=f32)
    else:
        tn = o_ref.shape[-1]
        for p in range(parts):
            o_ref[p] = jnp.dot(xn, w_ref[:, p * tn:(p + 1) * tn], preferred_element_type=f32)


def _norm_matmul(x, g, w, l, *, parts=1):
    M, K = x.shape
    N = w.shape[2]
    tn = N // parts
    rows = max(8, (12 * 1024 * 1024) // (4 * N))
    tm = _pick(M, tuple(t for t in (1024, 512, 256, 128, 64, 32, 16, 8) if t <= rows))
    if parts == 1:
        out_shape = jax.ShapeDtypeStruct((M, N), f32)
        out_spec = pl.BlockSpec((tm, N), lambda i: (i, 0))
    else:
        out_shape = jax.ShapeDtypeStruct((parts, M, tn), f32)
        out_spec = pl.BlockSpec((parts, tm, tn), lambda i: (0, i, 0))
    return pl.pallas_call(
        functools.partial(_norm_matmul_kernel, parts=parts),
        grid=(M // tm,),
        in_specs=[pl.BlockSpec((tm, K), lambda i: (i, 0)),
                  pl.BlockSpec((None, 1, K), lambda i: (l, 0, 0)),
                  pl.BlockSpec((None, K, N), lambda i: (l, 0, 0), pipeline_mode=pl.Buffered(1))],
        out_specs=out_spec,
        out_shape=out_shape,
        compiler_params=_cp("parallel"),
        name="norm_matmul",
    )(x, g, w)


def _two_src(tm, width, n_p):
    return (pl.BlockSpec((tm, width), lambda m: (jnp.minimum(m, n_p - 1), 0)),
            pl.BlockSpec((tm, width), lambda m: (jnp.maximum(m - n_p, 0), 0)))


def _merge_kernel(h_ref, g_ref, *refs, n_p):
    yp = refs[0:N_BRANCH]
    ys = refs[N_BRANCH:2 * N_BRANCH]
    wg_ref, bg_ref, wb_ref, wo_ref, o_ref = refs[2 * N_BRANCH:]
    m = pl.program_id(0)
    h = h_ref[...]
    D = h.shape[1]
    xn = _rms(h, g_ref[...]).astype(bf16)
    merged = None
    for k in range(N_BRANCH):
        gate = jax.nn.sigmoid(
            jnp.dot(xn, wg_ref[:, k * D:(k + 1) * D], preferred_element_type=f32) + bg_ref[k])
        y = jnp.where(m < n_p, yp[k][...].astype(bf16), ys[k][...].astype(bf16))
        term = gate * jnp.dot(y, wb_ref[k], preferred_element_type=f32)
        merged = term if merged is None else merged + term
    o_ref[...] = h + jnp.dot(merged.astype(bf16), wo_ref[...], preferred_element_type=f32)


def _merge(h, g, yps, yss, w_gate, b_gate, w_branch, w_out, l):
    T, D = h.shape
    Tp, Ts = yps[0].shape[0], yss[0].shape[0]
    tm = _pick(math.gcd(Tp, Ts), (512, 256, 128, 64, 32, 16, 8))
    n_p = Tp // tm
    sp, ss = _two_src(tm, BW, n_p)
    resident = lambda shape: pl.BlockSpec((None,) + shape, lambda m: (l,) + (0,) * len(shape),
                                          pipeline_mode=pl.Buffered(1))
    return pl.pallas_call(
        functools.partial(_merge_kernel, n_p=n_p),
        grid=(T // tm,),
        in_specs=[pl.BlockSpec((tm, D), lambda m: (m, 0)),
                  pl.BlockSpec((None, 1, D), lambda m: (l, 0, 0))]
                 + [sp] * N_BRANCH + [ss] * N_BRANCH
                 + [resident((D, N_BRANCH * D)), resident((N_BRANCH, 1, D)), resident((N_BRANCH, BW, D)),
                    resident((D, D))],
        out_specs=pl.BlockSpec((tm, D), lambda m: (m, 0)),
        out_shape=jax.ShapeDtypeStruct((T, D), f32),
        compiler_params=_cp("parallel"),
        name="branch_merge",
    )(h, g, *yps, *yss, w_gate, b_gate, w_branch, w_out)


def _xo_kernel(h_ref, op_ref, os_ref, w_ref, o_ref, *, n_p):
    m = pl.program_id(0)
    o = jnp.where(m < n_p, op_ref[...].astype(bf16), os_ref[...].astype(bf16))
    o_ref[...] = h_ref[...] + jnp.dot(o, w_ref[...], preferred_element_type=f32)


def _xo(h, o_p, o_s, w, l):
    T, D = h.shape
    Tp, Ts = o_p.shape[0], o_s.shape[0]
    tm = _pick(math.gcd(Tp, Ts), (1024, 512, 256, 128, 64, 32, 16, 8))
    n_p = Tp // tm
    sp, ss = _two_src(tm, D, n_p)
    return pl.pallas_call(
        functools.partial(_xo_kernel, n_p=n_p),
        grid=(T // tm,),
        in_specs=[pl.BlockSpec((tm, D), lambda m: (m, 0)), sp, ss,
                  pl.BlockSpec((None, D, D), lambda m: (l, 0, 0))],
        out_specs=pl.BlockSpec((tm, D), lambda m: (m, 0)),
        out_shape=jax.ShapeDtypeStruct((T, D), f32),
        compiler_params=_cp("parallel"),
        name="xattn_out",
    )(h, o_p, o_s, w)


def _ffn_kernel(h_ref, g_ref, wu_ref, wd_ref, o_ref, *, tf):
    h = h_ref[...]
    xn = _rms(h, g_ref[...]).astype(bf16)
    d_ff = wd_ref.shape[0]
    out = h
    for j in range(d_ff // tf):
        a = jnp.dot(xn, wu_ref[:, j * tf:(j + 1) * tf], preferred_element_type=f32)
        b = jnp.dot(xn, wu_ref[:, d_ff + j * tf:d_ff + (j + 1) * tf], preferred_element_type=f32)
        out = out + jnp.dot((_silu(a) * b).astype(bf16), wd_ref[j * tf:(j + 1) * tf, :],
                            preferred_element_type=f32)
    o_ref[...] = out


def _ffn(h, g, w_up, w_down, l):
    T, D = h.shape
    d_ff = w_down.shape[1]
    tm = _pick(T, (512, 256, 128, 64, 32, 16, 8))
    tf = _pick(d_ff, (256, 128))
    resident = lambda shape: pl.BlockSpec((None,) + shape, lambda m: (l,) + (0,) * len(shape),
                                          pipeline_mode=pl.Buffered(1))
    return pl.pallas_call(
        functools.partial(_ffn_kernel, tf=tf),
        grid=(T // tm,),
        in_specs=[pl.BlockSpec((tm, D), lambda m: (m, 0)),
                  pl.BlockSpec((None, 1, D), lambda m: (l, 0, 0)),
                  resident((D, 2 * d_ff)), resident((d_ff, D))],
        out_specs=pl.BlockSpec((tm, D), lambda m: (m, 0)),
        out_shape=jax.ShapeDtypeStruct((T, D), f32),
        compiler_params=_cp("parallel"),
        name="swiglu_ffn",
    )(h, g, w_up, w_down)


def _final_norm_kernel(x_ref, g_ref, o_ref):
    o_ref[...] = _rms(x_ref[...], g_ref[...])


def _final_norm(h, g):
    T, D = h.shape
    tm = _pick(T, (1024, 512, 256, 128, 64, 32, 16, 8))
    return pl.pallas_call(
        _final_norm_kernel,
        grid=(T // tm,),
        in_specs=[pl.BlockSpec((tm, D), lambda m: (m, 0)), pl.BlockSpec((1, D), lambda m: (0, 0))],
        out_specs=pl.BlockSpec((tm, D), lambda m: (m, 0)),
        out_shape=jax.ShapeDtypeStruct((T, D), f32),
        compiler_params=_cp("parallel"),
        name="final_norm",
    )(h, g)


def _units_per_step(B, nchunks):
    if nchunks == 1:
        return _pick(B, (8, 4, 2, 1)), 1
    return 1, _pick(nchunks, (CHUNKS_PER_STEP, 2, 1))


def _ret_kernel(q_ref, k_ref, v_ref, g_ref, cos_ref, sin_ref, s0_ref, ng_ref, y_ref, sf_ref, S_ref,
                *, C, nch, SB, CB):
    n = pl.program_id(1)

    @pl.when(n == 0)
    def _():
        S_ref[...] = s0_ref[...]

    lane = lax.broadcasted_iota(jnp.int32, (C, H_A * DK_A), 1)
    first_half = (lane % DK_A) < (DK_A // 2)

    def rot(x, cos, sin):
        fwd = pltpu.roll(x, DK_A // 2, axis=1)
        bwd = pltpu.roll(x, H_A * DK_A - DK_A // 2, axis=1)
        return x * cos + jnp.where(first_half, bwd, fwd) * sin

    ng = ng_ref[...]
    t_col = lax.broadcasted_iota(jnp.int32, (C, 1), 0).astype(f32)
    diff = (lax.broadcasted_iota(jnp.int32, (C, C), 0)
            - lax.broadcasted_iota(jnp.int32, (C, C), 1))
    for u in range(SB * CB):
        s, cb = divmod(u, CB)
        rs = slice(u * C, (u + 1) * C)
        cos = cos_ref[cb * C:(cb + 1) * C, :]
        sin = sin_ref[cb * C:(cb + 1) * C, :]
        q = rot(q_ref[rs, :], cos, sin)
        k = rot(k_ref[rs, :], cos, sin) * (DK_A ** -0.5)
        v = v_ref[rs, :]
        gate = g_ref[rs, :]
        for h in range(H_A):
            lg = math.log1p(-2.0 ** (-5.0 - h))
            qh = q[:, h * DK_A:(h + 1) * DK_A]
            kh = k[:, h * DK_A:(h + 1) * DK_A]
            vs = slice(h * DV_A, (h + 1) * DV_A)
            vh = v[:, vs]
            decay = jnp.where(diff >= 0, jnp.exp(diff.astype(f32) * lg), 0.0)
            scores = _dot_nt(qh, kh) * decay
            s_prev = S_ref[s, h]
            o = _dot(scores, vh) + _dot(qh * jnp.exp((t_col + 1.0) * lg), s_prev)
            k_out = kh * jnp.exp((C - 1.0 - t_col) * lg)
            S_ref[s, h] = math.exp(C * lg) * s_prev + _dot_tn(k_out, vh)
            o = o - jnp.mean(o, axis=-1, keepdims=True)
            o = o * lax.rsqrt(jnp.mean(o * o, axis=-1, keepdims=True) + EPS)
            y_ref[rs, vs] = (_silu(gate[:, vs]) * (o * ng[:, vs])).astype(y_ref.dtype)

    @pl.when(n == nch - 1)
    def _():
        sf_ref[...] = S_ref[...]


def _retention(proj, cos, sin, s0, ls, ng, l, *, B, L, row_off, out_dtype):
    C = math.gcd(L, CHUNK)
    SB, CB = _units_per_step(B, L // C)
    nch = L // (C * CB)
    R = SB * CB * C
    r0 = row_off // R
    row = lambda b, n: r0 + b * nch + n
    return pl.pallas_call(
        functools.partial(_ret_kernel, C=C, nch=nch, SB=SB, CB=CB),
        grid=(B // SB, nch),
        in_specs=[pl.BlockSpec((R, 256), lambda b, n: (row(b, n), C_AQ)),
                  pl.BlockSpec((R, 256), lambda b, n: (row(b, n), C_AK)),
                  pl.BlockSpec((R, 512), lambda b, n: (row(b, n), C_AV)),
                  pl.BlockSpec((R, 512), lambda b, n: (row(b, n), C_AG)),
                  pl.BlockSpec((CB * C, 256), lambda b, n: (n, 0)),
                  pl.BlockSpec((CB * C, 256), lambda b, n: (n, 0)),
                  pl.BlockSpec((None, SB, H_A, DK_A, DV_A), lambda b, n: (ls, b, 0, 0, 0)),
                  pl.BlockSpec((None, 1, BW), lambda b, n: (l, 0, 0))],
        out_specs=[pl.BlockSpec((R, BW), lambda b, n: (b * nch + n, 0)),
                   pl.BlockSpec((SB, H_A, DK_A, DV_A), lambda b, n: (b, 0, 0, 0))],
        out_shape=[jax.ShapeDtypeStruct((B * L, BW), out_dtype),
                   jax.ShapeDtypeStruct((B, H_A, DK_A, DV_A), f32)],
        scratch_shapes=[pltpu.VMEM((SB, H_A, DK_A, DV_A), f32)],
        compiler_params=_cp("parallel", "arbitrary"),
        name="retention",
    )(proj, proj, proj, proj, cos, sin, s0, ng)


def _ssd_kernel(z_ref, xs_ref, bc_ref, sm_ref, c0_ref, s0_ref, cw_ref, cb_ref, dtb_ref, al_ref,
                dsk_ref, ng_ref, y_ref, cn_ref, sf_ref, xpad_ref, S_ref, o_ref, *, C, nch, SB, CB):
    n = pl.program_id(1)
    P0 = 8 - (CONV_W - 1)

    @pl.when(n == 0)
    def _():
        xpad_ref[:, P0:8, :] = c0_ref[...]
        S_ref[...] = s0_ref[...]

    cw = cw_ref[...]
    dsk = dsk_ref[...]
    causal = (lax.broadcasted_iota(jnp.int32, (C, C), 0)
              >= lax.broadcasted_iota(jnp.int32, (C, C), 1))
    rep = H_B // G_B
    for u in range(SB * CB):
        s = u // CB
        rs = slice(u * C, (u + 1) * C)
        xpad_ref[s, 8:8 + C, 0:BW] = xs_ref[rs, :]
        xpad_ref[s, 8:8 + C, BW:CONV_DIM] = bc_ref[rs, :]
        conv = cb_ref[...] + xpad_ref[s, P0:P0 + C, :] * cw[0:1, :]
        for j in range(1, CONV_W):
            conv = conv + xpad_ref[s, P0 + j:P0 + j + C, :] * cw[j:j + 1, :]
        tail = xpad_ref[s, C + P0:C + 8, :]
        cn_ref[s] = tail
        xpad_ref[s, P0:8, :] = tail

        act = _silu(conv)
        xs = act[:, 0:BW]
        dt = _softplus(sm_ref[rs, :] + dtb_ref[...])
        la = dt * (-jnp.exp(al_ref[...]))
        cum = _cumsum_rows(la)
        cum_t = cum.T
        cum_last = cum[C - 1:C, :]
        for g in range(G_B):
            bm = act[:, BW + g * N_B:BW + (g + 1) * N_B]
            cm = act[:, BW + G_B * N_B + g * N_B:BW + G_B * N_B + (g + 1) * N_B]
            scores_g = _dot_nt(cm, bm)
            for h in range(g * rep, (g + 1) * rep):
                ps = slice(h * P_B, (h + 1) * P_B)
                cum_h = cum[:, h:h + 1]
                seg = cum_h - cum_t[h:h + 1, :]
                decay = jnp.exp(jnp.where(causal, seg, -jnp.inf))
                xh = xs[:, ps]
                vh = xh * dt[:, h:h + 1]
                s_prev = S_ref[s, h]
                o = _dot(scores_g * decay, vh) + _dot(cm * jnp.exp(cum_h), s_prev)
                k_out = bm * jnp.exp(cum_last[:, h:h + 1] - cum_h)
                S_ref[s, h] = jnp.exp(cum_last[:, h:h + 1]) * s_prev + _dot_tn(k_out, vh)
                o_ref[u, :, ps] = o + dsk[:, ps] * xh
        y = o_ref[u] * _silu(z_ref[rs, :])
        y_ref[rs, :] = _rms(y, ng_ref[...]).astype(y_ref.dtype)

    @pl.when(n == nch - 1)
    def _():
        sf_ref[...] = S_ref[...]


def _ssd(proj, c0, s0, ls, cw, cb, dtb, al, dsk, ng, l, *, B, L, row_off, out_dtype):
    C = math.gcd(L, CHUNK)
    SB, CB = _units_per_step(B, L // C)
    nch = L // (C * CB)
    R = SB * CB * C
    r0 = row_off // R
    row = lambda b, n: r0 + b * nch + n
    par = lambda shape: pl.BlockSpec((None,) + shape, lambda b, n: (l,) + (0,) * len(shape))
    return pl.pallas_call(
        functools.partial(_ssd_kernel, C=C, nch=nch, SB=SB, CB=CB),
        grid=(B // SB, nch),
        in_specs=[pl.BlockSpec((R, 512), lambda b, n: (row(b, n), C_BZ)),
                  pl.BlockSpec((R, 512), lambda b, n: (row(b, n), C_BXS)),
                  pl.BlockSpec((R, 256), lambda b, n: (row(b, n), C_BBC)),
                  pl.BlockSpec((R, 128), lambda b, n: (row(b, n), C_SMALL)),
                  pl.BlockSpec((None, SB, CONV_W - 1, CONV_DIM), lambda b, n: (ls, b, 0, 0)),
                  pl.BlockSpec((None, SB, H_B, N_B, P_B), lambda b, n: (ls, b, 0, 0, 0)),
                  par((CONV_W, CONV_DIM)), par((1, CONV_DIM)), par((1, LANE)), par((1, LANE)),
                  par((1, BW)), par((1, BW))],
        out_specs=[pl.BlockSpec((R, BW), lambda b, n: (b * nch + n, 0)),
                   pl.BlockSpec((SB, CONV_W - 1, CONV_DIM), lambda b, n: (b, 0, 0)),
                   pl.BlockSpec((SB, H_B, N_B, P_B), lambda b, n: (b, 0, 0, 0))],
        out_shape=[jax.ShapeDtypeStruct((B * L, BW), out_dtype),
                   jax.ShapeDtypeStruct((B, CONV_W - 1, CONV_DIM), f32),
                   jax.ShapeDtypeStruct((B, H_B, N_B, P_B), f32)],
        scratch_shapes=[pltpu.VMEM((SB, C + 8, CONV_DIM), f32), pltpu.VMEM((SB, H_B, N_B, P_B), f32),
                        pltpu.VMEM((SB * CB, C, BW), f32)],
        compiler_params=_cp("parallel", "arbitrary"),
        name="ssd",
    )(proj, proj, proj, proj, c0, s0, cw, cb, dtb, al, dsk, ng)


def _gla_kernel(q_ref, k_ref, v_ref, r_ref, sm_ref, s0_ref, w2_ref, b2_ref, ng_ref, y_ref, sf_ref,
                S_ref, *, C, c, nch, SB, CB):
    n = pl.program_id(1)

    @pl.when(n == 0)
    def _():
        S_ref[...] = s0_ref[...]

    x_all = jnp.dot(sm_ref[...].astype(bf16), w2_ref[...], preferred_element_type=f32) + b2_ref[...]
    la_all = _log_sigmoid(x_all) / GLA_TAU
    for u in range(SB * CB):
        rs = slice(u * C, (u + 1) * C)
        _gla_one_chunk(la_all[rs, :], q_ref[rs, :], k_ref[rs, :], v_ref[rs, :], r_ref[rs, :], ng_ref[...],
                       y_ref, rs, S_ref, u // CB, C=C, c=c)

    @pl.when(n == nch - 1)
    def _():
        sf_ref[...] = S_ref[...]


def _gla_one_chunk(la, q, k, v, r, ng, y_ref, rs, S_ref, s, *, C, c):
    nsub = C // c
    HK = H_D * DK_D
    cum = _cumsum_rows(la)
    cum_last = cum[C - 1:C, :]
    qs = q * (DK_D ** -0.5)
    q_in = qs * jnp.exp(cum)
    k_out = k * jnp.exp(cum_last - cum)
    dec_col = jnp.broadcast_to(jnp.exp(cum_last), (8, HK)).T[:, 0:1]

    row = lax.broadcasted_iota(jnp.int32, (C, LANE), 0)
    lane = lax.broadcasted_iota(jnp.int32, (C, LANE), 1)
    causal = (lax.broadcasted_iota(jnp.int32, (C, C), 0)
              >= lax.broadcasted_iota(jnp.int32, (C, C), 1))
    for p in range(HK // LANE):
        sl = slice(p * LANE, (p + 1) * LANE)
        cum_p = cum[:, sl]
        k_p = k[:, sl]
        ref_row = jnp.zeros((C, LANE), f32)
        k_pieces = []
        for i in range(nsub):
            ref_i = cum_p[i * c - 1:i * c, :] if i > 0 else jnp.zeros((1, LANE), f32)
            in_blk = (row >= i * c) & (row < (i + 1) * c)
            ref_row = jnp.where(in_blk, ref_i, ref_row)
            expo = jnp.where(row < (i + 1) * c, ref_i - cum_p, -jnp.inf)
            k_pieces.append((k_p * jnp.exp(expo)).astype(bf16))
        k_big = jnp.concatenate(k_pieces, axis=1) if nsub > 1 else k_pieces[0]
        q_sc = qs[:, sl] * jnp.exp(cum_p - ref_row)
        for hh in range(LANE // DK_D):
            h = p * (LANE // DK_D) + hh
            in_head = (lane >= hh * DK_D) & (lane < (hh + 1) * DK_D)
            q_pieces = []
            for i in range(nsub):
                in_blk = (row >= i * c) & (row < (i + 1) * c)
                q_pieces.append(jnp.where(in_blk & in_head, q_sc, 0.0).astype(bf16))
            q_big = jnp.concatenate(q_pieces, axis=1) if nsub > 1 else q_pieces[0]
            scores = jnp.where(causal, _dot_nt(q_big, k_big), 0.0)
            vs = slice(h * DV_D, (h + 1) * DV_D)
            vh = v[:, vs]
            s_prev = S_ref[s, h]
            hs = slice(h * DK_D, (h + 1) * DK_D)
            o = _dot(scores, vh) + _dot(q_in[:, hs], s_prev)
            S_ref[s, h] = dec_col[hs, :] * s_prev + _dot_tn(k_out[:, hs], vh)
            o = o * lax.rsqrt(jnp.mean(o * o, axis=-1, keepdims=True) + EPS)
            y_ref[rs, vs] = (_silu(r[:, vs]) * (o * ng[:, vs])).astype(y_ref.dtype)


def _gla(proj, s0, ls, w2, b2, ng, l, *, B, L, row_off, out_dtype):
    C = math.gcd(L, CHUNK)
    c = math.gcd(L, GLA_SUB)
    SB, CB = _units_per_step(B, L // C)
    nch = L // (C * CB)
    R = SB * CB * C
    r0 = row_off // R
    row = lambda b, n: r0 + b * nch + n
    par = lambda shape: pl.BlockSpec((None,) + shape, lambda b, n: (l,) + (0,) * len(shape))
    return pl.pallas_call(
        functools.partial(_gla_kernel, C=C, c=c, nch=nch, SB=SB, CB=CB),
        grid=(B // SB, nch),
        in_specs=[pl.BlockSpec((R, 256), lambda b, n: (row(b, n), C_DQ)),
                  pl.BlockSpec((R, 256), lambda b, n: (row(b, n), C_DK)),
                  pl.BlockSpec((R, 512), lambda b, n: (row(b, n), C_DV)),
                  pl.BlockSpec((R, 512), lambda b, n: (row(b, n), C_DR)),
                  pl.BlockSpec((R, 128), lambda b, n: (row(b, n), C_SMALL)),
                  pl.BlockSpec((None, SB, H_D, DK_D, DV_D), lambda b, n: (ls, b, 0, 0, 0)),
                  par((LANE, H_D * DK_D)), par((1, H_D * DK_D)), par((1, BW))],
        out_specs=[pl.BlockSpec((R, BW), lambda b, n: (b * nch + n, 0)),
                   pl.BlockSpec((SB, H_D, DK_D, DV_D), lambda b, n: (b, 0, 0, 0))],
        out_shape=[jax.ShapeDtypeStruct((B * L, BW), out_dtype),
                   jax.ShapeDtypeStruct((B, H_D, DK_D, DV_D), f32)],
        scratch_shapes=[pltpu.VMEM((SB, H_D, DK_D, DV_D), f32)],
        compiler_params=_cp("parallel", "arbitrary"),
        name="gla",
    )(proj, proj, proj, proj, proj, s0, w2, b2, ng)


def _fox_prep_kernel(sm_ref, bf_ref, lf_ref, c_ref, *carry, C):
    chained = bool(carry)
    lf = _log_sigmoid(sm_ref[...] + bf_ref[...])
    lf = pltpu.roll(lf, LANE - H_C, axis=1)
    lane = lax.broadcasted_iota(jnp.int32, lf.shape, 1)
    lf = jnp.where(lane < H_C, lf, 0.0)
    cs = _cumsum_rows(lf, seg=C)
    if chained:
        carry_ref, = carry

        @pl.when(pl.program_id(1) == 0)
        def _():
            carry_ref[...] = jnp.zeros_like(carry_ref)

        cs = carry_ref[...] + cs
        carry_ref[...] = cs[C - 1:C, :]
    lf_ref[...] = lf[:, 0:H_C]
    c_ref[...] = cs


def _fox_prep(proj, bf, l, *, B, L, row_off):
    C = math.gcd(L, 4 * CHUNK)
    nch = L // C
    chained = nch > 1
    SB = 1 if chained else _pick(B, (16, 8, 4, 2, 1))
    R = SB * C
    r0 = row_off // R
    return pl.pallas_call(
        functools.partial(_fox_prep_kernel, C=C),
        grid=(B // SB, nch),
        in_specs=[pl.BlockSpec((R, LANE), lambda b, n: (r0 + b * nch + n, C_SMALL)),
                  pl.BlockSpec((None, 1, LANE), lambda b, n: (l, 0, 0))],
        out_specs=[pl.BlockSpec((R, H_C), lambda b, n: (b * nch + n, 0)),
                   pl.BlockSpec((R, LANE), lambda b, n: (b * nch + n, 0))],
        out_shape=[jax.ShapeDtypeStruct((B * L, H_C), f32), jax.ShapeDtypeStruct((B * L, LANE), f32)],
        scratch_shapes=[pltpu.VMEM((1, LANE), f32)] if chained else [],
        compiler_params=_cp("parallel", "arbitrary"),
        name="fox_prep",
    )(proj, bf)


def _fox_prompt_kernel(q_ref, k_ref, v_ref, c_ref, y_ref, kb_ref, vt_ref, qt_ref, m_ref, l_ref,
                       acc_ref, *, tq, tk):
    i = pl.program_id(1)

    @pl.when(i == 0)
    def _():
        kb_ref[...] = k_ref[...].astype(bf16)
        vt_ref[...] = v_ref[...].T.astype(bf16)

    qt_ref[...] = (q_ref[...] * (DH_C ** -0.5)).T.astype(bf16)
    m_ref[...] = jnp.full(m_ref.shape, -jnp.inf, f32)
    l_ref[...] = jnp.zeros(l_ref.shape, f32)
    acc_ref[...] = jnp.zeros(acc_ref.shape, f32)
    r = tq // tk
    key_minus_query = (lax.broadcasted_iota(jnp.int32, (tk, tq), 0)
                       - lax.broadcasted_iota(jnp.int32, (tk, tq), 1))

    def block(j, diag):
        start = pl.multiple_of(j * tk, tk)
        for h in range(H_C):
            hs = slice(h * DH_C, (h + 1) * DH_C)
            s = jnp.dot(kb_ref[pl.ds(start, tk), hs], qt_ref[hs, :], preferred_element_type=f32)
            s = s - c_ref[pl.ds(start, tk), h:h + 1]
            if diag is not None:
                s = jnp.where(key_minus_query <= -diag * tk, s, -jnp.inf)
            m_old = m_ref[h]
            m_new = jnp.maximum(m_old, jnp.max(s, axis=0, keepdims=True))
            p = jnp.exp(s - m_new)
            alpha = jnp.exp(m_old - m_new)
            l_ref[h] = alpha * l_ref[h] + jnp.sum(p, axis=0, keepdims=True)
            acc_ref[hs, :] = alpha * acc_ref[hs, :] + jnp.dot(
                vt_ref[hs, pl.ds(start, tk)], p.astype(bf16), preferred_element_type=f32)
            m_ref[h] = m_new

    def full_block(j, carry):
        block(j, None)
        return carry

    block(i * r, 0)
    lax.fori_loop(0, i * r, full_block, 0)
    for d in range(1, r):
        block(i * r + d, d)
    for h in range(H_C):
        hs = slice(h * DH_C, (h + 1) * DH_C)
        acc_ref[hs, :] = acc_ref[hs, :] / l_ref[h]
    y_ref[...] = acc_ref[...].T.astype(y_ref.dtype)


def _fox_prompt(proj, c, *, B, L, out_dtype, tq_pref=FOX_TQ, tk_pref=FOX_TK):
    tq = _pick(L, tuple(t for t in (2048, 1024, 512, 256, 128, 64, 32, 16, 8) if t <= tq_pref))
    tk = min(tq, tk_pref)
    nq = L // tq
    return pl.pallas_call(
        functools.partial(_fox_prompt_kernel, tq=tq, tk=tk),
        grid=(B, nq),
        in_specs=[pl.BlockSpec((tq, BW), lambda b, i: (b * nq + i, C_CQ)),
                  pl.BlockSpec((L, BW), lambda b, i: (b, C_CK)),
                  pl.BlockSpec((L, BW), lambda b, i: (b, C_CV)),
                  pl.BlockSpec((L, LANE), lambda b, i: (b, 0))],
        out_specs=pl.BlockSpec((tq, BW), lambda b, i: (b * nq + i, 0)),
        out_shape=jax.ShapeDtypeStruct((B * L, BW), out_dtype),
        scratch_shapes=[pltpu.VMEM((L, BW), bf16), pltpu.VMEM((BW, L), bf16), pltpu.VMEM((BW, tq), bf16),
                        pltpu.VMEM((H_C, 1, tq), f32), pltpu.VMEM((H_C, 1, tq), f32),
                        pltpu.VMEM((BW, tq), f32)],
        compiler_params=_cp("parallel", "arbitrary"),
        name="fox_prompt",
    )(proj, proj, proj, c)


def _fox_sample_kernel(pt_ref, q_ref, kn_ref, vn_ref, cn_ref, *refs, T, npages):
    kp = refs[0:npages]
    vp = refs[npages:2 * npages]
    lfp = refs[2 * npages:3 * npages]
    y_ref = refs[3 * npages]
    P = npages * PAGE_SIZE
    lf_t = jnp.concatenate([r[...] for r in lfp], axis=1)
    lane = lax.broadcasted_iota(jnp.int32, (H_C, P), 1)
    pre = lf_t
    k = 1
    while k < P:
        pre = pre + jnp.where(lane >= k, pltpu.roll(pre, k, axis=1), 0.0)
        k *= 2
    suffix = pre[:, P - 1:P] - pre
    q = q_ref[...] * (DH_C ** -0.5)
    kn = kn_ref[...]
    vn = vn_ref[...]
    cn_t = cn_ref[...].T[0:H_C, :]
    causal_new = (lax.broadcasted_iota(jnp.int32, (T, T), 1)
                  <= lax.broadcasted_iota(jnp.int32, (T, T), 0))
    s_parts, s_new_parts = [], []
    for h in range(H_C):
        hs = slice(h * DH_C, (h + 1) * DH_C)
        qh = q[:, hs].astype(bf16)
        k_t = jnp.concatenate([r[h].astype(bf16) for r in kp], axis=1)
        s_parts.append(_dot(qh, k_t) + suffix[h:h + 1, :])
        s_new_parts.append(jnp.where(causal_new, _dot_nt(qh, kn[:, hs]) - cn_t[h:h + 1, :], -jnp.inf))
    s = jnp.concatenate(s_parts, axis=0)
    s_new = jnp.concatenate(s_new_parts, axis=0)
    m = jnp.maximum(jnp.max(s, axis=-1, keepdims=True), jnp.max(s_new, axis=-1, keepdims=True))
    p = jnp.exp(s - m)
    p_new = jnp.exp(s_new - m)
    denom = jnp.sum(p, axis=-1, keepdims=True) + jnp.sum(p_new, axis=-1, keepdims=True)
    for h in range(H_C):
        hs = slice(h * DH_C, (h + 1) * DH_C)
        rows = slice(h * T, (h + 1) * T)
        v_t = jnp.concatenate([r[h].astype(bf16) for r in vp], axis=1)
        o = _dot_nt(p[rows, :], v_t) + _dot(p_new[rows, :], vn[:, hs])
        y_ref[:, hs] = (o / denom[rows, :]).astype(y_ref.dtype)


def _fox_sample(proj, cnew, cache_kt, cache_vt, cache_lft, page_table, l, *, B, T, row_off, out_dtype):
    npages = page_table.shape[1]
    r0 = row_off // T
    kv_specs = [pl.BlockSpec((None, None, H_C, DH_C, PAGE_SIZE), lambda b, pt, p=p: (l, pt[b, p], 0, 0, 0))
                for p in range(npages)]
    lf_specs = [pl.BlockSpec((None, None, H_C, PAGE_SIZE), lambda b, pt, p=p: (l, pt[b, p], 0, 0))
                for p in range(npages)]
    grid_spec = pltpu.PrefetchScalarGridSpec(
        num_scalar_prefetch=1,
        grid=(B,),
        in_specs=[pl.BlockSpec((T, BW), lambda b, pt: (r0 + b, C_CQ)),
                  pl.BlockSpec((T, BW), lambda b, pt: (r0 + b, C_CK)),
                  pl.BlockSpec((T, BW), lambda b, pt: (r0 + b, C_CV)),
                  pl.BlockSpec((T, LANE), lambda b, pt: (b, 0))] + kv_specs + kv_specs + lf_specs,
        out_specs=pl.BlockSpec((T, BW), lambda b, pt: (b, 0)),
    )
    return pl.pallas_call(
        functools.partial(_fox_sample_kernel, T=T, npages=npages),
        grid_spec=grid_spec,
        out_shape=jax.ShapeDtypeStruct((B * T, BW), out_dtype),
        compiler_params=_cp("parallel"),
        name="fox_sample",
    )(page_table, proj, proj, proj, cnew, *([cache_kt] * npages), *([cache_vt] * npages),
      *([cache_lft] * npages))


def _xattn_kernel(q_ref, k_ref, v_ref, o_ref):
    q = q_ref[...]
    dh = q.shape[-1] // H_X
    for h in range(H_X):
        hs = slice(h * dh, (h + 1) * dh)
        s = _dot_nt(q[:, hs], k_ref[:, hs]) * (dh ** -0.5)
        m = jnp.max(s, axis=-1, keepdims=True)
        p = jnp.exp(s - m)
        p = p / jnp.sum(p, axis=-1, keepdims=True)
        o_ref[:, hs] = _dot(p, v_ref[:, hs]).astype(o_ref.dtype)


def _xattn(xq, mem_k, mem_v, lk, lv, *, B, L, row_off, out_dtype):
    D = xq.shape[1]
    n_mem = mem_k.shape[2]
    tq = _pick(L, (512, 256, 128, 64, 32, 16, 8))
    nq = L // tq
    r0 = row_off // tq
    return pl.pallas_call(
        _xattn_kernel,
        grid=(B, nq),
        in_specs=[pl.BlockSpec((tq, D), lambda b, i: (r0 + b * nq + i, 0)),
                  pl.BlockSpec((None, None, n_mem, D), lambda b, i: (lk, b, 0, 0)),
                  pl.BlockSpec((None, None, n_mem, D), lambda b, i: (lv, b, 0, 0))],
        out_specs=pl.BlockSpec((tq, D), lambda b, i: (b * nq + i, 0)),
        out_shape=jax.ShapeDtypeStruct((B * L, D), out_dtype),
        compiler_params=_cp("parallel", "arbitrary"),
        name="xattn_core",
    )(xq, mem_k, mem_v)


def _fox_kv_out_kernel(*refs, depth):
    k_refs, v_refs = refs[0:depth], refs[depth:2 * depth]
    kt_ref, vt_ref = refs[2 * depth:]
    layer = pl.program_id(0)
    for j in range(depth):
        @pl.when(layer == j)
        def _(j=j):
            kt_ref[...] = k_refs[j][...].T.reshape(kt_ref.shape)
            vt_ref[...] = v_refs[j][...].T.reshape(vt_ref.shape)


def _fox_kv_out(projs, *, B, L):
    depth = len(projs)
    tl = _pick(L, (512, 256, 128))
    nl = L // tl

    def src(j, col):
        return pl.BlockSpec((tl, BW), lambda d, b, i: (jnp.where(d == j, b * nl + i, 0), col))

    out_spec = pl.BlockSpec((None, None, H_C, DH_C, tl), lambda d, b, i: (d, b, 0, 0, i))
    out_shape = jax.ShapeDtypeStruct((depth, B, H_C, DH_C, L), f32)
    return pl.pallas_call(
        functools.partial(_fox_kv_out_kernel, depth=depth),
        grid=(depth, B, nl),
        in_specs=[src(j, C_CK) for j in range(depth)] + [src(j, C_CV) for j in range(depth)],
        out_specs=[out_spec, out_spec],
        out_shape=[out_shape, out_shape],
        compiler_params=_cp("parallel", "parallel", "parallel"),
        name="fox_kv_out",
    )(*projs, *projs)


def _xattn_decode_kernel(q_ref, k_ref, v_ref, o_ref, *, SB, T):
    n_mem, dh = k_ref.shape[1], k_ref.shape[3]
    R, NK = H_X * T, n_mem * H_X
    same_head = (lax.broadcasted_iota(jnp.int32, (R, NK), 0) // T
                 == lax.broadcasted_iota(jnp.int32, (R, NK), 1) % H_X)
    for s in range(SB):
        rs = slice(s * T, (s + 1) * T)
        q = q_ref[rs, :]
        q_stack = jnp.concatenate([q[:, h * dh:(h + 1) * dh] for h in range(H_X)], axis=0)
        kf = k_ref[s].reshape(NK, dh)
        vf = v_ref[s].reshape(NK, dh)
        sc = jnp.where(same_head, _dot_nt(q_stack, kf) * (dh ** -0.5), -jnp.inf)
        p = jnp.exp(sc - jnp.max(sc, axis=-1, keepdims=True))
        p = p / jnp.sum(p, axis=-1, keepdims=True)
        o = _dot(p, vf)
        for h in range(H_X):
            o_ref[rs, h * dh:(h + 1) * dh] = o[h * T:(h + 1) * T, :].astype(o_ref.dtype)


def _xattn_decode(xq, mem_k, mem_v, l, *, B, T, row_off, out_dtype):
    D = xq.shape[1]
    SB = _pick(B, (4, 2, 1))
    R = SB * T
    r0 = row_off // R
    kv_spec = pl.BlockSpec((None, SB) + mem_k.shape[2:], lambda b: (l, b, 0, 0, 0))
    return pl.pallas_call(
        functools.partial(_xattn_decode_kernel, SB=SB, T=T),
        grid=(B // SB,),
        in_specs=[pl.BlockSpec((R, D), lambda b: (r0 + b, 0)), kv_spec, kv_spec],
        out_specs=pl.BlockSpec((R, D), lambda b: (b, 0)),
        out_shape=jax.ShapeDtypeStruct((B * T, D), out_dtype),
        compiler_params=_cp("parallel"),
        name="xattn_decode",
    )(xq, mem_k, mem_v)


def _rope_tables(pos):
    half = DK_A // 2
    inv_freq = 1.0 / (ROPE_BASE ** (jnp.arange(half, dtype=f32) / half))
    ang = pos.astype(f32)[:, None] * inv_freq[None, :]
    cos, sin = jnp.cos(ang), jnp.sin(ang)
    return (jnp.tile(jnp.concatenate([cos, cos], axis=1), (1, H_A)),
            jnp.tile(jnp.concatenate([-sin, sin], axis=1), (1, H_A)))


def _w_in_segments():
    names = ("a_q", "a_k", "a_v", "a_g", "b_z", "b_xs", "b_bc", "b_dt", "c_q", "c_k", "c_v", "c_f",
             "d_q", "d_k", "d_v", "d_r", "d_lr")
    sizes = (256, 256, 512, 512, 512, BW, CONV_DIM - BW, H_B, 512, 512, 512, H_C, 256, 256, 512, 512, GLA_RANK)
    src, off = {}, 0
    for n, s in zip(names, sizes):
        src[n] = (off, s)
        off += s
    wide = [src[n] for n in ("a_v", "a_g", "b_z", "b_xs", "c_q", "c_k", "c_v", "d_v", "d_r",
                             "a_q", "a_k", "b_bc", "d_q", "d_k")]
    narrow = [src[n] for n in ("b_dt", "c_f", "d_lr")]
    return wide, narrow, off


def _w_in_prep_kernel(w_ref, mix_ref, gate_ref):
    wide, narrow, gate_start = _w_in_segments()
    dst = 0
    for start, width in wide:
        mix_ref[:, dst:dst + width] = w_ref[:, start:start + width].astype(bf16)
        dst += width
    rows = w_ref.shape[0]
    lane = lax.broadcasted_iota(jnp.int32, (rows, LANE), 1)
    small = jnp.zeros((rows, LANE), f32)
    for start, width in narrow:
        base = start - start % LANE
        in_seg = (lane >= start - base) & (lane < start - base + width)
        small = jnp.where(in_seg, w_ref[:, base:base + LANE], small)
    mix_ref[:, dst:dst + LANE] = small.astype(bf16)
    mix_ref[:, dst + LANE:] = jnp.zeros((rows, mix_ref.shape[1] - dst - LANE), bf16)
    gate_ref[...] = w_ref[:, gate_start:].astype(bf16)


def _permute_w_in(w_in):
    depth, D, n_in = w_in.shape
    n_gate = N_BRANCH * D
    _, narrow, gate_start = _w_in_segments()
    assert [s % LANE for s, _ in narrow] == [0, H_B, H_B + H_C] and gate_start + n_gate == n_in
    tk = _pick(D, (256, 128, 64, 32, 16))
    return pl.pallas_call(
        _w_in_prep_kernel,
        grid=(depth, D // tk),
        in_specs=[pl.BlockSpec((None, tk, n_in), lambda d, i: (d, i, 0))],
        out_specs=[pl.BlockSpec((None, tk, NP_COLS), lambda d, i: (d, i, 0)),
                   pl.BlockSpec((None, tk, n_gate), lambda d, i: (d, i, 0))],
        out_shape=[jax.ShapeDtypeStruct((depth, D, NP_COLS), bf16),
                   jax.ShapeDtypeStruct((depth, D, n_gate), bf16)],
        compiler_params=_cp("parallel", "parallel"),
        name="w_in_layout",
    )(w_in)


def _pad_lanes(x, start):
    depth, n = x.shape
    return jnp.zeros((depth, 1, LANE), f32).at[:, 0, start:start + n].set(x)


def kernel(x_prompt, x_sample, cache_fox_k, cache_fox_v, cache_fox_logf, cache_mem_k, cache_mem_v, state_ret, state_ssm, state_conv, state_gla, page_table, mem_prompt, w_in, b_gate, ret_norm_g, ssd_conv_w, ssd_conv_b, ssd_dt_bias, ssd_a_log, ssd_d, ssd_norm_g, fox_bf, gla_w_lr2, gla_b_lr, gla_norm_g, w_branch, w_out, norm_mix_g, norm_x_g, norm_mem_g, w_xq, w_mem_kv, w_xo, norm_ffn_g, w_ffn_up, w_ffn_down, norm_final_g):
    Bp, Lp, D = x_prompt.shape
    Bs, Ls, _ = x_sample.shape
    depth = w_in.shape[0]
    n_mem = mem_prompt.shape[1]
    past_len = page_table.shape[1] * PAGE_SIZE
    Tp, Ts = Bp * Lp, Bs * Ls

    w_mix, w_gate = _permute_w_in(w_in)
    w_branch_b = w_branch.astype(bf16)
    w_out_b = w_out.astype(bf16)
    w_xq_b = w_xq.astype(bf16)
    w_xo_b = w_xo.astype(bf16)
    w_kv_b = w_mem_kv.astype(bf16)
    w_up_b = w_ffn_up.astype(bf16)
    w_down_b = w_ffn_down.astype(bf16)
    b_gate4 = b_gate.reshape(depth, N_BRANCH, 1, D)
    g_mix = norm_mix_g.reshape(depth, 1, D)
    g_x = norm_x_g.reshape(depth, 1, D)
    g_mem = norm_mem_g.reshape(depth, 1, D)
    g_ffn = norm_ffn_g.reshape(depth, 1, D)
    ret_g = ret_norm_g.reshape(depth, 1, BW)
    gla_g = gla_norm_g.reshape(depth, 1, BW)
    ssd_g = ssd_norm_g.reshape(depth, 1, BW)
    conv_b = ssd_conv_b.reshape(depth, 1, CONV_DIM)
    dt_bias = _pad_lanes(ssd_dt_bias, 0)
    a_log = _pad_lanes(ssd_a_log, 0)
    d_skip = jnp.repeat(ssd_d, P_B, axis=1).reshape(depth, 1, BW)
    fox_b = _pad_lanes(fox_bf, H_B)
    w_lr2 = jnp.zeros((depth, LANE, H_D * DK_D), f32).at[:, H_B + H_C:H_B + H_C + GLA_RANK, :].set(
        gla_w_lr2).astype(bf16)
    b_lr = gla_b_lr.reshape(depth, 1, H_D * DK_D)
    cos_p, sin_p = _rope_tables(jnp.arange(Lp, dtype=jnp.int32))
    cos_s, sin_s = _rope_tables(past_len + jnp.arange(Ls, dtype=jnp.int32))
    ck_t = jnp.transpose(cache_fox_k, (0, 1, 3, 4, 2))
    cv_t = jnp.transpose(cache_fox_v, (0, 1, 3, 4, 2))
    clf_t = jnp.transpose(cache_fox_logf, (0, 1, 3, 2))
    z_ret = jnp.zeros((1, Bp, H_A, DK_A, DV_A), f32)
    z_ssm = jnp.zeros((1, Bp, H_B, N_B, P_B), f32)
    z_conv = jnp.zeros((1, Bp, CONV_W - 1, CONV_DIM), f32)
    z_gla = jnp.zeros((1, Bp, H_D, DK_D, DV_D), f32)
    mem2d = mem_prompt.reshape(Bp * n_mem, D)

    h = jnp.concatenate([x_prompt.reshape(Tp, D), x_sample.reshape(Ts, D)], axis=0)
    outs = {k: [] for k in ("ret_p", "ssm_p", "conv_p", "gla_p", "flf_p", "mk_p", "mv_p",
                            "ret_s", "ssm_s", "conv_s", "gla_s", "fk_s", "fv_s", "flf_s")}
    projs = []
    for l in range(depth):
        proj = _norm_matmul(h, g_mix, w_mix, l)

        pk = dict(B=Bp, L=Lp, row_off=0, out_dtype=bf16)
        sk = dict(B=Bs, L=Ls, row_off=Tp, out_dtype=f32)
        ya_p, ret_p = _retention(proj, cos_p, sin_p, z_ret, 0, ret_g, l, **pk)
        ya_s, ret_s = _retention(proj, cos_s, sin_s, state_ret, l, ret_g, l, **sk)
        yb_p, conv_p, ssm_p = _ssd(proj, z_conv, z_ssm, 0, ssd_conv_w, conv_b, dt_bias, a_log, d_skip,
                                   ssd_g, l, **pk)
        yb_s, conv_s, ssm_s = _ssd(proj, state_conv, state_ssm, l, ssd_conv_w, conv_b, dt_bias, a_log,
                                   d_skip, ssd_g, l, **sk)
        lf_p, c_p = _fox_prep(proj, fox_b, l, B=Bp, L=Lp, row_off=0)
        lf_s, c_s = _fox_prep(proj, fox_b, l, B=Bs, L=Ls, row_off=Tp)
        yc_p = _fox_prompt(proj, c_p, B=Bp, L=Lp, out_dtype=bf16)
        yc_s = _fox_sample(proj, c_s, ck_t, cv_t, clf_t, page_table, l, B=Bs, T=Ls, row_off=Tp, out_dtype=f32)
        yd_p, gla_p = _gla(proj, z_gla, 0, w_lr2, b_lr, gla_g, l, **pk)
        yd_s, gla_s = _gla(proj, state_gla, l, w_lr2, b_lr, gla_g, l, **sk)

        h = _merge(h, g_mix, (ya_p, yb_p, yc_p, yd_p), (ya_s, yb_s, yc_s, yd_s),
                   w_gate, b_gate4, w_branch_b, w_out_b, l)

        kv = _norm_matmul(mem2d, g_mem, w_kv_b, l, parts=2)
        mem_kv = kv.reshape(2, Bp, n_mem, D)
        xq = _norm_matmul(h, g_x, w_xq_b, l)
        o_p = _xattn(xq, mem_kv, mem_kv, 0, 1, B=Bp, L=Lp, row_off=0, out_dtype=bf16)
        o_s = _xattn_decode(xq, cache_mem_k, cache_mem_v, l, B=Bs, T=Ls, row_off=Tp, out_dtype=f32)
        h = _xo(h, o_p, o_s, w_xo_b, l)
        h = _ffn(h, g_ffn, w_up_b, w_down_b, l)

        projs.append(proj)
        ckv = proj[Tp:, C_CK * BW:(C_CV + 1) * BW]
        outs["ret_p"].append(ret_p); outs["ssm_p"].append(ssm_p); outs["conv_p"].append(conv_p)
        outs["gla_p"].append(gla_p)
        outs["flf_p"].append(lf_p.reshape(Bp, Lp, H_C))
        outs["mk_p"].append(kv[0].reshape(Bp, n_mem, H_X, D // H_X))
        outs["mv_p"].append(kv[1].reshape(Bp, n_mem, H_X, D // H_X))
        outs["ret_s"].append(ret_s); outs["ssm_s"].append(ssm_s); outs["conv_s"].append(conv_s)
        outs["gla_s"].append(gla_s)
        outs["fk_s"].append(ckv[:, :BW].reshape(Bs, Ls, H_C, DH_C))
        outs["fv_s"].append(ckv[:, BW:].reshape(Bs, Ls, H_C, DH_C))
        outs["flf_s"].append(lf_s.reshape(Bs, Ls, H_C))

    y = _final_norm(h, norm_final_g.reshape(1, D))
    st = {k: jnp.stack(v, axis=0) for k, v in outs.items()}
    fk_t, fv_t = _fox_kv_out(projs, B=Bp, L=Lp)
    st["fk_p"] = jnp.transpose(fk_t, (0, 1, 4, 2, 3))
    st["fv_p"] = jnp.transpose(fv_t, (0, 1, 4, 2, 3))
    return (y[:Tp].reshape(Bp, Lp, D), y[Tp:].reshape(Bs, Ls, D),
            st["ret_p"], st["ssm_p"], st["conv_p"], st["gla_p"], st["fk_p"], st["fv_p"], st["flf_p"],
            st["mk_p"], st["mv_p"],
            st["ret_s"], st["ssm_s"], st["conv_s"], st["gla_s"], st["fk_s"], st["fv_s"], st["flf_s"])
```

```python
import functools
import math

import jax
import jax.numpy as jnp
from jax import lax
from jax.experimental import pallas as pl
from jax.experimental.pallas import tpu as pltpu

f32 = jnp.float32
bf16 = jnp.bfloat16

EPS = 1e-6
CHUNK = 128
GLA_CHUNK = 16
GLA_SUB = 16
CHUNKS_PER_STEP = 4
FOX_TQ, FOX_TK = 1024, 1024
ROPE_BASE = 10000.0
PAGE_SIZE = 128
H_A, DK_A, DV_A = 4, 64, 128
H_B, P_B, N_B, G_B, CONV_W = 8, 64, 64, 2, 4
H_C, DH_C = 8, 64
H_D, DK_D, DV_D = 4, 64, 128
GLA_RANK = 16
GLA_TAU = 16.0
N_BRANCH = 4
H_X = 4
BW = 512
CONV_DIM = BW + 2 * G_B * N_B

C_AV, C_AG, C_BZ, C_BXS, C_CQ, C_CK, C_CV, C_DV, C_DR = range(9)
C_AQ, C_AK, C_BBC, C_DQ, C_DK = range(18, 23)
C_SMALL = 46
NP_COLS = 6144
LANE = 128
VMEM_LIMIT = 56 * 1024 * 1024


def _cp(*sem):
    return pltpu.CompilerParams(dimension_semantics=sem, vmem_limit_bytes=VMEM_LIMIT)


def _pick(n, cands):
    for c in cands:
        if n % c == 0:
            return c
    raise ValueError(f"no tile for {n}")


def _dot(a, b):
    return jnp.dot(a.astype(bf16), b.astype(bf16), preferred_element_type=f32)


def _dot_nt(a, b):
    return lax.dot_general(a.astype(bf16), b.astype(bf16), (((1,), (1,)), ((), ())),
                           preferred_element_type=f32)


def _dot_tn(a, b):
    return lax.dot_general(a.astype(bf16), b.astype(bf16), (((0,), (0,)), ((), ())),
                           preferred_element_type=f32)


def _silu(x):
    return x * jax.nn.sigmoid(x)


def _log_sigmoid(x):
    return jnp.minimum(x, 0.0) - jnp.log1p(jnp.exp(-jnp.abs(x)))


def _softplus(x):
    return jnp.maximum(x, 0.0) + jnp.log1p(jnp.exp(-jnp.abs(x)))


def _cumsum_rows(x, seg=None):
    n = x.shape[0] if seg is None else seg
    row = lax.broadcasted_iota(jnp.int32, x.shape, 0)
    if n != x.shape[0]:
        row = row % n
    k = 1
    while k < n:
        x = x + jnp.where(row >= k, pltpu.roll(x, k, axis=0), 0.0)
        k *= 2
    return x


def _rms(x, g):
    return x * lax.rsqrt(jnp.mean(x * x, axis=-1, keepdims=True) + EPS) * g


def _norm_matmul_kernel(x_ref, g_ref, w_ref, o_ref, *, parts):
    xn = _rms(x_ref[...], g_ref[...]).astype(bf16)
    if parts == 1:
        o_ref[...] = jnp.dot(xn, w_ref[...], preferred_element_type=f32)
    else:
        tn = o_ref.shape[-1]
        for p in range(parts):
            o_ref[p] = jnp.dot(xn, w_ref[:, p * tn:(p + 1) * tn], preferred_element_type=f32)


def _norm_matmul(x, g, w, l, *, parts=1):
    M, K = x.shape
    N = w.shape[2]
    tn = N // parts
    rows = max(8, (12 * 1024 * 1024) // (4 * N))
    tm = _pick(M, tuple(t for t in (1024, 512, 256, 128, 64, 32, 16, 8) if t <= rows))
    if parts == 1:
        out_shape = jax.ShapeDtypeStruct((M, N), f32)
        out_spec = pl.BlockSpec((tm, N), lambda i: (i, 0))
    else:
        out_shape = jax.ShapeDtypeStruct((parts, M, tn), f32)
        out_spec = pl.BlockSpec((parts, tm, tn), lambda i: (0, i, 0))
    return pl.pallas_call(
        functools.partial(_norm_matmul_kernel, parts=parts),
        grid=(M // tm,),
        in_specs=[pl.BlockSpec((tm, K), lambda i: (i, 0)),
                  pl.BlockSpec((None, 1, K), lambda i: (l, 0, 0)),
                  pl.BlockSpec((None, K, N), lambda i: (l, 0, 0), pipeline_mode=pl.Buffered(1))],
        out_specs=out_spec,
        out_shape=out_shape,
        compiler_params=_cp("parallel"),
        name="norm_matmul",
    )(x, g, w)


def _two_src(tm, width, n_p):
    return (pl.BlockSpec((tm, width), lambda m: (jnp.minimum(m, n_p - 1), 0)),
            pl.BlockSpec((tm, width), lambda m: (jnp.maximum(m - n_p, 0), 0)))


def _merge_kernel(h_ref, g_ref, *refs, n_p):
    yp = refs[0:N_BRANCH]
    ys = refs[N_BRANCH:2 * N_BRANCH]
    wg_ref, bg_ref, wb_ref, wo_ref, o_ref = refs[2 * N_BRANCH:]
    m = pl.program_id(0)
    h = h_ref[...]
    D = h.shape[1]
    xn = _rms(h, g_ref[...]).astype(bf16)
    merged = None
    for k in range(N_BRANCH):
        gate = jax.nn.sigmoid(
            jnp.dot(xn, wg_ref[:, k * D:(k + 1) * D], preferred_element_type=f32) + bg_ref[k])
        y = jnp.where(m < n_p, yp[k][...].astype(bf16), ys[k][...].astype(bf16))
        term = gate * jnp.dot(y, wb_ref[k], preferred_element_type=f32)
        merged = term if merged is None else merged + term
    o_ref[...] = h + jnp.dot(merged.astype(bf16), wo_ref[...], preferred_element_type=f32)


def _merge(h, g, yps, yss, w_gate, b_gate, w_branch, w_out, l):
    T, D = h.shape
    Tp, Ts = yps[0].shape[0], yss[0].shape[0]
    tm = _pick(math.gcd(Tp, Ts), (512, 256, 128, 64, 32, 16, 8))
    n_p = Tp // tm
    sp, ss = _two_src(tm, BW, n_p)
    resident = lambda shape: pl.BlockSpec((None,) + shape, lambda m: (l,) + (0,) * len(shape),
                                          pipeline_mode=pl.Buffered(1))
    return pl.pallas_call(
        functools.partial(_merge_kernel, n_p=n_p),
        grid=(T // tm,),
        in_specs=[pl.BlockSpec((tm, D), lambda m: (m, 0)),
                  pl.BlockSpec((None, 1, D), lambda m: (l, 0, 0))]
                 + [sp] * N_BRANCH + [ss] * N_BRANCH
                 + [resident((D, N_BRANCH * D)), resident((N_BRANCH, 1, D)), resident((N_BRANCH, BW, D)),
                    resident((D, D))],
        out_specs=pl.BlockSpec((tm, D), lambda m: (m, 0)),
        out_shape=jax.ShapeDtypeStruct((T, D), f32),
        compiler_params=_cp("parallel"),
        name="branch_merge",
    )(h, g, *yps, *yss, w_gate, b_gate, w_branch, w_out)


def _xo_kernel(h_ref, op_ref, os_ref, w_ref, o_ref, *, n_p):
    m = pl.program_id(0)
    o = jnp.where(m < n_p, op_ref[...].astype(bf16), os_ref[...].astype(bf16))
    o_ref[...] = h_ref[...] + jnp.dot(o, w_ref[...], preferred_element_type=f32)


def _xo(h, o_p, o_s, w, l):
    T, D = h.shape
    Tp, Ts = o_p.shape[0], o_s.shape[0]
    tm = _pick(math.gcd(Tp, Ts), (1024, 512, 256, 128, 64, 32, 16, 8))
    n_p = Tp // tm
    sp, ss = _two_src(tm, D, n_p)
    return pl.pallas_call(
        functools.partial(_xo_kernel, n_p=n_p),
        grid=(T // tm,),
        in_specs=[pl.BlockSpec((tm, D), lambda m: (m, 0)), sp, ss,
                  pl.BlockSpec((None, D, D), lambda m: (l, 0, 0))],
        out_specs=pl.BlockSpec((tm, D), lambda m: (m, 0)),
        out_shape=jax.ShapeDtypeStruct((T, D), f32),
        compiler_params=_cp("parallel"),
        name="xattn_out",
    )(h, o_p, o_s, w)


def _ffn_kernel(h_ref, g_ref, wu_ref, wd_ref, o_ref, *, tf):
    h = h_ref[...]
    xn = _rms(h, g_ref[...]).astype(bf16)
    d_ff = wd_ref.shape[0]
    out = h
    for j in range(d_ff // tf):
        a = jnp.dot(xn, wu_ref[:, j * tf:(j + 1) * tf], preferred_element_type=f32)
        b = jnp.dot(xn, wu_ref[:, d_ff + j * tf:d_ff + (j + 1) * tf], preferred_element_type=f32)
        out = out + jnp.dot((_silu(a) * b).astype(bf16), wd_ref[j * tf:(j + 1) * tf, :],
                            preferred_element_type=f32)
    o_ref[...] = out


def _ffn(h, g, w_up, w_down, l):
    T, D = h.shape
    d_ff = w_down.shape[1]
    tm = _pick(T, (512, 256, 128, 64, 32, 16, 8))
    tf = _pick(d_ff, (256, 128))
    resident = lambda shape: pl.BlockSpec((None,) + shape, lambda m: (l,) + (0,) * len(shape),
                                          pipeline_mode=pl.Buffered(1))
    return pl.pallas_call(
        functools.partial(_ffn_kernel, tf=tf),
        grid=(T // tm,),
        in_specs=[pl.BlockSpec((tm, D), lambda m: (m, 0)),
                  pl.BlockSpec((None, 1, D), lambda m: (l, 0, 0)),
                  resident((D, 2 * d_ff)), resident((d_ff, D))],
        out_specs=pl.BlockSpec((tm, D), lambda m: (m, 0)),
        out_shape=jax.ShapeDtypeStruct((T, D), f32),
        compiler_params=_cp("parallel"),
        name="swiglu_ffn",
    )(h, g, w_up, w_down)


def _final_norm_kernel(x_ref, g_ref, o_ref):
    o_ref[...] = _rms(x_ref[...], g_ref[...])


def _final_norm(h, g):
    T, D = h.shape
    tm = _pick(T, (1024, 512, 256, 128, 64, 32, 16, 8))
    return pl.pallas_call(
        _final_norm_kernel,
        grid=(T // tm,),
        in_specs=[pl.BlockSpec((tm, D), lambda m: (m, 0)), pl.BlockSpec((1, D), lambda m: (0, 0))],
        out_specs=pl.BlockSpec((tm, D), lambda m: (m, 0)),
        out_shape=jax.ShapeDtypeStruct((T, D), f32),
        compiler_params=_cp("parallel"),
        name="final_norm",
    )(h, g)


def _units_per_step(B, nchunks):
    if nchunks == 1:
        return _pick(B, (8, 4, 2, 1)), 1
    return 1, _pick(nchunks, (CHUNKS_PER_STEP, 2, 1))


def _ret_kernel(q_ref, k_ref, v_ref, g_ref, cos_ref, sin_ref, s0_ref, ng_ref, y_ref, sf_ref, S_ref,
                *, C, nch, SB, CB):
    n = pl.program_id(1)

    @pl.when(n == 0)
    def _():
        S_ref[...] = s0_ref[...]

    lane = lax.broadcasted_iota(jnp.int32, (C, H_A * DK_A), 1)
    first_half = (lane % DK_A) < (DK_A // 2)

    def rot(x, cos, sin):
        fwd = pltpu.roll(x, DK_A // 2, axis=1)
        bwd = pltpu.roll(x, H_A * DK_A - DK_A // 2, axis=1)
        return x * cos + jnp.where(first_half, bwd, fwd) * sin

    ng = ng_ref[...]
    t_col = lax.broadcasted_iota(jnp.int32, (C, 1), 0).astype(f32)
    diff = (lax.broadcasted_iota(jnp.int32, (C, C), 0)
            - lax.broadcasted_iota(jnp.int32, (C, C), 1))
    for u in range(SB * CB):
        s, cb = divmod(u, CB)
        rs = slice(u * C, (u + 1) * C)
        cos = cos_ref[cb * C:(cb + 1) * C, :]
        sin = sin_ref[cb * C:(cb + 1) * C, :]
        q = rot(q_ref[rs, :], cos, sin)
        k = rot(k_ref[rs, :], cos, sin) * (DK_A ** -0.5)
        v = v_ref[rs, :]
        gate = g_ref[rs, :]
        for h in range(H_A):
            lg = math.log1p(-2.0 ** (-5.0 - h))
            qh = q[:, h * DK_A:(h + 1) * DK_A]
            kh = k[:, h * DK_A:(h + 1) * DK_A]
            vs = slice(h * DV_A, (h + 1) * DV_A)
            vh = v[:, vs]
            decay = jnp.where(diff >= 0, jnp.exp(diff.astype(f32) * lg), 0.0)
            scores = _dot_nt(qh, kh) * decay
            s_prev = S_ref[s, h]
            o = _dot(scores, vh) + _dot(qh * jnp.exp((t_col + 1.0) * lg), s_prev)
            k_out = kh * jnp.exp((C - 1.0 - t_col) * lg)
            S_ref[s, h] = math.exp(C * lg) * s_prev + _dot_tn(k_out, vh)
            o = o - jnp.mean(o, axis=-1, keepdims=True)
            o = o * lax.rsqrt(jnp.mean(o * o, axis=-1, keepdims=True) + EPS)
            y_ref[rs, vs] = (_silu(gate[:, vs]) * (o * ng[:, vs])).astype(y_ref.dtype)

    @pl.when(n == nch - 1)
    def _():
        sf_ref[...] = S_ref[...]


def _retention(proj, cos, sin, s0, ls, ng, l, *, B, L, row_off, out_dtype):
    C = math.gcd(L, CHUNK)
    SB, CB = _units_per_step(B, L // C)
    nch = L // (C * CB)
    R = SB * CB * C
    r0 = row_off // R
    row = lambda b, n: r0 + b * nch + n
    return pl.pallas_call(
        functools.partial(_ret_kernel, C=C, nch=nch, SB=SB, CB=CB),
        grid=(B // SB, nch),
        in_specs=[pl.BlockSpec((R, 256), lambda b, n: (row(b, n), C_AQ)),
                  pl.BlockSpec((R, 256), lambda b, n: (row(b, n), C_AK)),
                  pl.BlockSpec((R, 512), lambda b, n: (row(b, n), C_AV)),
                  pl.BlockSpec((R, 512), lambda b, n: (row(b, n), C_AG)),
                  pl.BlockSpec((CB * C, 256), lambda b, n: (n, 0)),
                  pl.BlockSpec((CB * C, 256), lambda b, n: (n, 0)),
                  pl.BlockSpec((None, SB, H_A, DK_A, DV_A), lambda b, n: (ls, b, 0, 0, 0)),
                  pl.BlockSpec((None, 1, BW), lambda b, n: (l, 0, 0))],
        out_specs=[pl.BlockSpec((R, BW), lambda b, n: (b * nch + n, 0)),
                   pl.BlockSpec((SB, H_A, DK_A, DV_A), lambda b, n: (b, 0, 0, 0))],
        out_shape=[jax.ShapeDtypeStruct((B * L, BW), out_dtype),
                   jax.ShapeDtypeStruct((B, H_A, DK_A, DV_A), f32)],
        scratch_shapes=[pltpu.VMEM((SB, H_A, DK_A, DV_A), f32)],
        compiler_params=_cp("parallel", "arbitrary"),
        name="retention",
    )(proj, proj, proj, proj, cos, sin, s0, ng)


def _ssd_kernel(z_ref, xs_ref, bc_ref, sm_ref, c0_ref, s0_ref, cw_ref, cb_ref, dtb_ref, al_ref,
                dsk_ref, ng_ref, y_ref, cn_ref, sf_ref, xpad_ref, S_ref, o_ref, *, C, nch, SB, CB):
    n = pl.program_id(1)
    P0 = 8 - (CONV_W - 1)

    @pl.when(n == 0)
    def _():
        xpad_ref[:, P0:8, :] = c0_ref[...]
        S_ref[...] = s0_ref[...]

    cw = cw_ref[...]
    dsk = dsk_ref[...]
    causal = (lax.broadcasted_iota(jnp.int32, (C, C), 0)
              >= lax.broadcasted_iota(jnp.int32, (C, C), 1))
    rep = H_B // G_B
    for u in range(SB * CB):
        s = u // CB
        rs = slice(u * C, (u + 1) * C)
        xpad_ref[s, 8:8 + C, 0:BW] = xs_ref[rs, :]
        xpad_ref[s, 8:8 + C, BW:CONV_DIM] = bc_ref[rs, :]
        conv = cb_ref[...] + xpad_ref[s, P0:P0 + C, :] * cw[0:1, :]
        for j in range(1, CONV_W):
            conv = conv + xpad_ref[s, P0 + j:P0 + j + C, :] * cw[j:j + 1, :]
        tail = xpad_ref[s, C + P0:C + 8, :]
        cn_ref[s] = tail
        xpad_ref[s, P0:8, :] = tail

        act = _silu(conv)
        xs = act[:, 0:BW]
        dt = _softplus(sm_ref[rs, :] + dtb_ref[...])
        la = dt * (-jnp.exp(al_ref[...]))
        cum = _cumsum_rows(la)
        cum_t = cum.T
        cum_last = cum[C - 1:C, :]
        for g in range(G_B):
            bm = act[:, BW + g * N_B:BW + (g + 1) * N_B]
            cm = act[:, BW + G_B * N_B + g * N_B:BW + G_B * N_B + (g + 1) * N_B]
            scores_g = _dot_nt(cm, bm)
            for h in range(g * rep, (g + 1) * rep):
                ps = slice(h * P_B, (h + 1) * P_B)
                cum_h = cum[:, h:h + 1]
                seg = cum_h - cum_t[h:h + 1, :]
                decay = jnp.exp(jnp.where(causal, seg, -jnp.inf))
                xh = xs[:, ps]
                vh = xh * dt[:, h:h + 1]
                s_prev = S_ref[s, h]
                o = _dot(scores_g * decay, vh) + _dot(cm * jnp.exp(cum_h), s_prev)
                k_out = bm * jnp.exp(cum_last[:, h:h + 1] - cum_h)
                S_ref[s, h] = jnp.exp(cum_last[:, h:h + 1]) * s_prev + _dot_tn(k_out, vh)
                o_ref[u, :, ps] = o + dsk[:, ps] * xh
        y = o_ref[u] * _silu(z_ref[rs, :])
        y_ref[rs, :] = _rms(y, ng_ref[...]).astype(y_ref.dtype)

    @pl.when(n == nch - 1)
    def _():
        sf_ref[...] = S_ref[...]


def _ssd(proj, c0, s0, ls, cw, cb, dtb, al, dsk, ng, l, *, B, L, row_off, out_dtype):
    C = math.gcd(L, CHUNK)
    SB, CB = _units_per_step(B, L // C)
    nch = L // (C * CB)
    R = SB * CB * C
    r0 = row_off // R
    row = lambda b, n: r0 + b * nch + n
    par = lambda shape: pl.BlockSpec((None,) + shape, lambda b, n: (l,) + (0,) * len(shape))
    return pl.pallas_call(
        functools.partial(_ssd_kernel, C=C, nch=nch, SB=SB, CB=CB),
        grid=(B // SB, nch),
        in_specs=[pl.BlockSpec((R, 512), lambda b, n: (row(b, n), C_BZ)),
                  pl.BlockSpec((R, 512), lambda b, n: (row(b, n), C_BXS)),
                  pl.BlockSpec((R, 256), lambda b, n: (row(b, n), C_BBC)),
                  pl.BlockSpec((R, 128), lambda b, n: (row(b, n), C_SMALL)),
                  pl.BlockSpec((None, SB, CONV_W - 1, CONV_DIM), lambda b, n: (ls, b, 0, 0)),
                  pl.BlockSpec((None, SB, H_B, N_B, P_B), lambda b, n: (ls, b, 0, 0, 0)),
                  par((CONV_W, CONV_DIM)), par((1, CONV_DIM)), par((1, LANE)), par((1, LANE)),
                  par((1, BW)), par((1, BW))],
        out_specs=[pl.BlockSpec((R, BW), lambda b, n: (b * nch + n, 0)),
                   pl.BlockSpec((SB, CONV_W - 1, CONV_DIM), lambda b, n: (b, 0, 0)),
                   pl.BlockSpec((SB, H_B, N_B, P_B), lambda b, n: (b, 0, 0, 0))],
        out_shape=[jax.ShapeDtypeStruct((B * L, BW), out_dtype),
                   jax.ShapeDtypeStruct((B, CONV_W - 1, CONV_DIM), f32),
                   jax.ShapeDtypeStruct((B, H_B, N_B, P_B), f32)],
        scratch_shapes=[pltpu.VMEM((SB, C + 8, CONV_DIM), f32), pltpu.VMEM((SB, H_B, N_B, P_B), f32),
                        pltpu.VMEM((SB * CB, C, BW), f32)],
        compiler_params=_cp("parallel", "arbitrary"),
        name="ssd",
    )(proj, proj, proj, proj, c0, s0, cw, cb, dtb, al, dsk, ng)


def _gla_kernel(q_ref, k_ref, v_ref, r_ref, sm_ref, s0_ref, w2_ref, b2_ref, ng_ref, y_ref, sf_ref,
                S_ref, *, C, c, nch, SB, CB):
    n = pl.program_id(1)

    @pl.when(n == 0)
    def _():
        S_ref[...] = s0_ref[...]

    x_all = jnp.dot(sm_ref[...].astype(bf16), w2_ref[...], preferred_element_type=f32) + b2_ref[...]
    la_all = _log_sigmoid(x_all) / GLA_TAU
    for u in range(SB * CB):
        rs = slice(u * C, (u + 1) * C)
        _gla_one_chunk(la_all[rs, :], q_ref[rs, :], k_ref[rs, :], v_ref[rs, :], r_ref[rs, :], ng_ref[...],
                       y_ref, rs, S_ref, u // CB, C=C, c=c)

    @pl.when(n == nch - 1)
    def _():
        sf_ref[...] = S_ref[...]


def _gla_one_chunk(la, q, k, v, r, ng, y_ref, rs, S_ref, s, *, C, c):
    nsub = C // c
    HK = H_D * DK_D
    cum = _cumsum_rows(la)
    cum_last = cum[C - 1:C, :]
    qs = q * (DK_D ** -0.5)
    q_in = qs * jnp.exp(cum)
    k_out = k * jnp.exp(cum_last - cum)
    dec_col = jnp.broadcast_to(jnp.exp(cum_last), (8, HK)).T[:, 0:1]

    row = lax.broadcasted_iota(jnp.int32, (C, LANE), 0)
    lane = lax.broadcasted_iota(jnp.int32, (C, LANE), 1)
    causal = (lax.broadcasted_iota(jnp.int32, (C, C), 0)
              >= lax.broadcasted_iota(jnp.int32, (C, C), 1))
    for p in range(HK // LANE):
        sl = slice(p * LANE, (p + 1) * LANE)
        cum_p = cum[:, sl]
        k_p = k[:, sl]
        ref_row = jnp.zeros((C, LANE), f32)
        k_pieces = []
        for i in range(nsub):
            ref_i = cum_p[i * c - 1:i * c, :] if i > 0 else jnp.zeros((1, LANE), f32)
            in_blk = (row >= i * c) & (row < (i + 1) * c)
            ref_row = jnp.where(in_blk, ref_i, ref_row)
            expo = jnp.where(row < (i + 1) * c, ref_i - cum_p, -jnp.inf)
            k_pieces.append((k_p * jnp.exp(expo)).astype(bf16))
        k_big = jnp.concatenate(k_pieces, axis=1) if nsub > 1 else k_pieces[0]
        q_sc = qs[:, sl] * jnp.exp(cum_p - ref_row)
        for hh in range(LANE // DK_D):
            h = p * (LANE // DK_D) + hh
            in_head = (lane >= hh * DK_D) & (lane < (hh + 1) * DK_D)
            q_pieces = []
            for i in range(nsub):
                in_blk = (row >= i * c) & (row < (i + 1) * c)
                q_pieces.append(jnp.where(in_blk & in_head, q_sc, 0.0).astype(bf16))
            q_big = jnp.concatenate(q_pieces, axis=1) if nsub > 1 else q_pieces[0]
            scores = jnp.where(causal, _dot_nt(q_big, k_big), 0.0)
            vs = slice(h * DV_D, (h + 1) * DV_D)
            vh = v[:, vs]
            s_prev = S_ref[s, h]
            hs = slice(h * DK_D, (h + 1) * DK_D)
            o = _dot(scores, vh) + _dot(q_in[:, hs], s_prev)
            S_ref[s, h] = dec_col[hs, :] * s_prev + _dot_tn(k_out[:, hs], vh)
            o = o * lax.rsqrt(jnp.mean(o * o, axis=-1, keepdims=True) + EPS)
            y_ref[rs, vs] = (_silu(r[:, vs]) * (o * ng[:, vs])).astype(y_ref.dtype)


def _gla(proj, s0, ls, w2, b2, ng, l, *, B, L, row_off, out_dtype):
    C = math.gcd(L, CHUNK)
    c = math.gcd(L, GLA_SUB)
    SB, CB = _units_per_step(B, L // C)
    nch = L // (C * CB)
    R = SB * CB * C
    r0 = row_off // R
    row = lambda b, n: r0 + b * nch + n
    par = lambda shape: pl.BlockSpec((None,) + shape, lambda b, n: (l,) + (0,) * len(shape))
    return pl.pallas_call(
        functools.partial(_gla_kernel, C=C, c=c, nch=nch, SB=SB, CB=CB),
        grid=(B // SB, nch),
        in_specs=[pl.BlockSpec((R, 256), lambda b, n: (row(b, n), C_DQ)),
                  pl.BlockSpec((R, 256), lambda b, n: (row(b, n), C_DK)),
                  pl.BlockSpec((R, 512), lambda b, n: (row(b, n), C_DV)),
                  pl.BlockSpec((R, 512), lambda b, n: (row(b, n), C_DR)),
                  pl.BlockSpec((R, 128), lambda b, n: (row(b, n), C_SMALL)),
                  pl.BlockSpec((None, SB, H_D, DK_D, DV_D), lambda b, n: (ls, b, 0, 0, 0)),
                  par((LANE, H_D * DK_D)), par((1, H_D * DK_D)), par((1, BW))],
        out_specs=[pl.BlockSpec((R, BW), lambda b, n: (b * nch + n, 0)),
                   pl.BlockSpec((SB, H_D, DK_D, DV_D), lambda b, n: (b, 0, 0, 0))],
        out_shape=[jax.ShapeDtypeStruct((B * L, BW), out_dtype),
                   jax.ShapeDtypeStruct((B, H_D, DK_D, DV_D), f32)],
        scratch_shapes=[pltpu.VMEM((SB, H_D, DK_D, DV_D), f32)],
        compiler_params=_cp("parallel", "arbitrary"),
        name="gla",
    )(proj, proj, proj, proj, proj, s0, w2, b2, ng)


def _fox_prep_kernel(sm_ref, bf_ref, lf_ref, c_ref, *carry, C):
    chained = bool(carry)
    lf = _log_sigmoid(sm_ref[...] + bf_ref[...])
    lf = pltpu.roll(lf, LANE - H_C, axis=1)
    lane = lax.broadcasted_iota(jnp.int32, lf.shape, 1)
    lf = jnp.where(lane < H_C, lf, 0.0)
    cs = _cumsum_rows(lf, seg=C)
    if chained:
        carry_ref, = carry

        @pl.when(pl.program_id(1) == 0)
        def _():
            carry_ref[...] = jnp.zeros_like(carry_ref)

        cs = carry_ref[...] + cs
        carry_ref[...] = cs[C - 1:C, :]
    lf_ref[...] = lf[:, 0:H_C]
    c_ref[...] = cs


def _fox_prep(proj, bf, l, *, B, L, row_off):
    C = math.gcd(L, 4 * CHUNK)
    nch = L // C
    chained = nch > 1
    SB = 1 if chained else _pick(B, (16, 8, 4, 2, 1))
    R = SB * C
    r0 = row_off // R
    return pl.pallas_call(
        functools.partial(_fox_prep_kernel, C=C),
        grid=(B // SB, nch),
        in_specs=[pl.BlockSpec((R, LANE), lambda b, n: (r0 + b * nch + n, C_SMALL)),
                  pl.BlockSpec((None, 1, LANE), lambda b, n: (l, 0, 0))],
        out_specs=[pl.BlockSpec((R, H_C), lambda b, n: (b * nch + n, 0)),
                   pl.BlockSpec((R, LANE), lambda b, n: (b * nch + n, 0))],
        out_shape=[jax.ShapeDtypeStruct((B * L, H_C), f32), jax.ShapeDtypeStruct((B * L, LANE), f32)],
        scratch_shapes=[pltpu.VMEM((1, LANE), f32)] if chained else [],
        compiler_params=_cp("parallel", "arbitrary"),
        name="fox_prep",
    )(proj, bf)


def _fox_prompt_kernel(q_ref, k_ref, v_ref, c_ref, y_ref, kb_ref, vt_ref, qt_ref, m_ref, l_ref,
                       acc_ref, *, tq, tk):
    i = pl.program_id(1)

    @pl.when(i == 0)
    def _():
        kb_ref[...] = k_ref[...].astype(bf16)
        vt_ref[...] = v_ref[...].T.astype(bf16)

    qt_ref[...] = (q_ref[...] * (DH_C ** -0.5)).T.astype(bf16)
    m_ref[...] = jnp.full(m_ref.shape, -jnp.inf, f32)
    l_ref[...] = jnp.zeros(l_ref.shape, f32)
    acc_ref[...] = jnp.zeros(acc_ref.shape, f32)
    r = tq // tk
    key_minus_query = (lax.broadcasted_iota(jnp.int32, (tk, tq), 0)
                       - lax.broadcasted_iota(jnp.int32, (tk, tq), 1))

    def block(j, diag):
        start = pl.multiple_of(j * tk, tk)
        for h in range(H_C):
            hs = slice(h * DH_C, (h + 1) * DH_C)
            s = jnp.dot(kb_ref[pl.ds(start, tk), hs], qt_ref[hs, :], preferred_element_type=f32)
            s = s - c_ref[pl.ds(start, tk), h:h + 1]
            if diag is not None:
                s = jnp.where(key_minus_query <= -diag * tk, s, -jnp.inf)
            m_old = m_ref[h]
            m_new = jnp.maximum(m_old, jnp.max(s, axis=0, keepdims=True))
            p = jnp.exp(s - m_new)
            alpha = jnp.exp(m_old - m_new)
            l_ref[h] = alpha * l_ref[h] + jnp.sum(p, axis=0, keepdims=True)
            acc_ref[hs, :] = alpha * acc_ref[hs, :] + jnp.dot(
                vt_ref[hs, pl.ds(start, tk)], p.astype(bf16), preferred_element_type=f32)
            m_ref[h] = m_new

    def full_block(j, carry):
        block(j, None)
        return carry

    block(i * r, 0)
    lax.fori_loop(0, i * r, full_block, 0)
    for d in range(1, r):
        block(i * r + d, d)
    for h in range(H_C):
        hs = slice(h * DH_C, (h + 1) * DH_C)
        acc_ref[hs, :] = acc_ref[hs, :] / l_ref[h]
    y_ref[...] = acc_ref[...].T.astype(y_ref.dtype)


def _fox_prompt(proj, c, *, B, L, out_dtype, tq_pref=FOX_TQ, tk_pref=FOX_TK):
    tq = _pick(L, tuple(t for t in (2048, 1024, 512, 256, 128, 64, 32, 16, 8) if t <= tq_pref))
    tk = min(tq, tk_pref)
    nq = L // tq
    return pl.pallas_call(
        functools.partial(_fox_prompt_kernel, tq=tq, tk=tk),
        grid=(B, nq),
        in_specs=[pl.BlockSpec((tq, BW), lambda b, i: (b * nq + i, C_CQ)),
                  pl.BlockSpec((L, BW), lambda b, i: (b, C_CK)),
                  pl.BlockSpec((L, BW), lambda b, i: (b, C_CV)),
                  pl.BlockSpec((L, LANE), lambda b, i: (b, 0))],
        out_specs=pl.BlockSpec((tq, BW), lambda b, i: (b * nq + i, 0)),
        out_shape=jax.ShapeDtypeStruct((B * L, BW), out_dtype),
        scratch_shapes=[pltpu.VMEM((L, BW), bf16), pltpu.VMEM((BW, L), bf16), pltpu.VMEM((BW, tq), bf16),
                        pltpu.VMEM((H_C, 1, tq), f32), pltpu.VMEM((H_C, 1, tq), f32),
                        pltpu.VMEM((BW, tq), f32)],
        compiler_params=_cp("parallel", "arbitrary"),
        name="fox_prompt",
    )(proj, proj, proj, c)


def _fox_sample_kernel(pt_ref, q_ref, kn_ref, vn_ref, cn_ref, *refs, T, npages):
    kp = refs[0:npages]
    vp = refs[npages:2 * npages]
    lfp = refs[2 * npages:3 * npages]
    y_ref = refs[3 * npages]
    P = npages * PAGE_SIZE
    lf_t = jnp.concatenate([r[...] for r in lfp], axis=1)
    lane = lax.broadcasted_iota(jnp.int32, (H_C, P), 1)
    pre = lf_t
    k = 1
    while k < P:
        pre = pre + jnp.where(lane >= k, pltpu.roll(pre, k, axis=1), 0.0)
        k *= 2
    suffix = pre[:, P - 1:P] - pre
    q = q_ref[...] * (DH_C ** -0.5)
    kn = kn_ref[...]
    vn = vn_ref[...]
    cn_t = cn_ref[...].T[0:H_C, :]
    causal_new = (lax.broadcasted_iota(jnp.int32, (T, T), 1)
                  <= lax.broadcasted_iota(jnp.int32, (T, T), 0))
    s_parts, s_new_parts = [], []
    for h in range(H_C):
        hs = slice(h * DH_C, (h + 1) * DH_C)
        qh = q[:, hs].astype(bf16)
        k_t = jnp.concatenate([r[h].astype(bf16) for r in kp], axis=1)
        s_parts.append(_dot(qh, k_t) + suffix[h:h + 1, :])
        s_new_parts.append(jnp.where(causal_new, _dot_nt(qh, kn[:, hs]) - cn_t[h:h + 1, :], -jnp.inf))
    s = jnp.concatenate(s_parts, axis=0)
    s_new = jnp.concatenate(s_new_parts, axis=0)
    m = jnp.maximum(jnp.max(s, axis=-1, keepdims=True), jnp.max(s_new, axis=-1, keepdims=True))
    p = jnp.exp(s - m)
    p_new = jnp.exp(s_new - m)
    denom = jnp.sum(p, axis=-1, keepdims=True) + jnp.sum(p_new, axis=-1, keepdims=True)
    for h in range(H_C):
        hs = slice(h * DH_C, (h + 1) * DH_C)
        rows = slice(h * T, (h + 1) * T)
        v_t = jnp.concatenate([r[h].astype(bf16) for r in vp], axis=1)
        o = _dot_nt(p[rows, :], v_t) + _dot(p_new[rows, :], vn[:, hs])
        y_ref[:, hs] = (o / denom[rows, :]).astype(y_ref.dtype)


def _fox_sample(proj, cnew, cache_kt, cache_vt, cache_lft, page_table, l, *, B, T, row_off, out_dtype):
    npages = page_table.shape[1]
    r0 = row_off // T
    kv_specs = [pl.BlockSpec((None, None, H_C, DH_C, PAGE_SIZE), lambda b, pt, p=p: (l, pt[b, p], 0, 0, 0))
                for p in range(npages)]
    lf_specs = [pl.BlockSpec((None, None, H_C, PAGE_SIZE), lambda b, pt, p=p: (l, pt[b, p], 0, 0))
                for p in range(npages)]
    grid_spec = pltpu.PrefetchScalarGridSpec(
        num_scalar_prefetch=1,
        grid=(B,),
        in_specs=[pl.BlockSpec((T, BW), lambda b, pt: (r0 + b, C_CQ)),
                  pl.BlockSpec((T, BW), lambda b, pt: (r0 + b, C_CK)),
                  pl.BlockSpec((T, BW), lambda b, pt: (r0 + b, C_CV)),
                  pl.BlockSpec((T, LANE), lambda b, pt: (b, 0))] + kv_specs + kv_specs + lf_specs,
        out_specs=pl.BlockSpec((T, BW), lambda b, pt: (b, 0)),
    )
    return pl.pallas_call(
        functools.partial(_fox_sample_kernel, T=T, npages=npages),
        grid_spec=grid_spec,
        out_shape=jax.ShapeDtypeStruct((B * T, BW), out_dtype),
        compiler_params=_cp("parallel"),
        name="fox_sample",
    )(page_table, proj, proj, proj, cnew, *([cache_kt] * npages), *([cache_vt] * npages),
      *([cache_lft] * npages))


def _xattn_kernel(q_ref, k_ref, v_ref, o_ref):
    q = q_ref[...]
    dh = q.shape[-1] // H_X
    for h in range(H_X):
        hs = slice(h * dh, (h + 1) * dh)
        s = _dot_nt(q[:, hs], k_ref[:, hs]) * (dh ** -0.5)
        m = jnp.max(s, axis=-1, keepdims=True)
        p = jnp.exp(s - m)
        p = p / jnp.sum(p, axis=-1, keepdims=True)
        o_ref[:, hs] = _dot(p, v_ref[:, hs]).astype(o_ref.dtype)


def _xattn(xq, mem_k, mem_v, lk, lv, *, B, L, row_off, out_dtype):
    D = xq.shape[1]
    n_mem = mem_k.shape[2]
    tq = _pick(L, (512, 256, 128, 64, 32, 16, 8))
    nq = L // tq
    r0 = row_off // tq
    return pl.pallas_call(
        _xattn_kernel,
        grid=(B, nq),
        in_specs=[pl.BlockSpec((tq, D), lambda b, i: (r0 + b * nq + i, 0)),
                  pl.BlockSpec((None, None, n_mem, D), lambda b, i: (lk, b, 0, 0)),
                  pl.BlockSpec((None, None, n_mem, D), lambda b, i: (lv, b, 0, 0))],
        out_specs=pl.BlockSpec((tq, D), lambda b, i: (b * nq + i, 0)),
        out_shape=jax.ShapeDtypeStruct((B * L, D), out_dtype),
        compiler_params=_cp("parallel", "arbitrary"),
        name="xattn_core",
    )(xq, mem_k, mem_v)


def _fox_kv_out_kernel(*refs, depth):
    k_refs, v_refs = refs[0:depth], refs[depth:2 * depth]
    kt_ref, vt_ref = refs[2 * depth:]
    layer = pl.program_id(0)
    for j in range(depth):
        @pl.when(layer == j)
        def _(j=j):
            kt_ref[...] = k_refs[j][...].T.reshape(kt_ref.shape)
            vt_ref[...] = v_refs[j][...].T.reshape(vt_ref.shape)


def _fox_kv_out(projs, *, B, L):
    depth = len(projs)
    tl = _pick(L, (512, 256, 128))
    nl = L // tl

    def src(j, col):
        return pl.BlockSpec((tl, BW), lambda d, b, i: (jnp.where(d == j, b * nl + i, 0), col))

    out_spec = pl.BlockSpec((None, None, H_C, DH_C, tl), lambda d, b, i: (d, b, 0, 0, i))
    out_shape = jax.ShapeDtypeStruct((depth, B, H_C, DH_C, L), f32)
    return pl.pallas_call(
        functools.partial(_fox_kv_out_kernel, depth=depth),
        grid=(depth, B, nl),
        in_specs=[src(j, C_CK) for j in range(depth)] + [src(j, C_CV) for j in range(depth)],
        out_specs=[out_spec, out_spec],
        out_shape=[out_shape, out_shape],
        compiler_params=_cp("parallel", "parallel", "parallel"),
        name="fox_kv_out",
    )(*projs, *projs)


def _xattn_decode_kernel(q_ref, k_ref, v_ref, o_ref, *, SB, T):
    n_mem, dh = k_ref.shape[1], k_ref.shape[3]
    R, NK = H_X * T, n_mem * H_X
    same_head = (lax.broadcasted_iota(jnp.int32, (R, NK), 0) // T
                 == lax.broadcasted_iota(jnp.int32, (R, NK), 1) % H_X)
    for s in range(SB):
        rs = slice(s * T, (s + 1) * T)
        q = q_ref[rs, :]
        q_stack = jnp.concatenate([q[:, h * dh:(h + 1) * dh] for h in range(H_X)], axis=0)
        kf = k_ref[s].reshape(NK, dh)
        vf = v_ref[s].reshape(NK, dh)
        sc = jnp.where(same_head, _dot_nt(q_stack, kf) * (dh ** -0.5), -jnp.inf)
        p = jnp.exp(sc - jnp.max(sc, axis=-1, keepdims=True))
        p = p / jnp.sum(p, axis=-1, keepdims=True)
        o = _dot(p, vf)
        for h in range(H_X):
            o_ref[rs, h * dh:(h + 1) * dh] = o[h * T:(h + 1) * T, :].astype(o_ref.dtype)


def _xattn_decode(xq, mem_k, mem_v, l, *, B, T, row_off, out_dtype):
    D = xq.shape[1]
    SB = _pick(B, (4, 2, 1))
    R = SB * T
    r0 = row_off // R
    kv_spec = pl.BlockSpec((None, SB) + mem_k.shape[2:], lambda b: (l, b, 0, 0, 0))
    return pl.pallas_call(
        functools.partial(_xattn_decode_kernel, SB=SB, T=T),
        grid=(B // SB,),
        in_specs=[pl.BlockSpec((R, D), lambda b: (r0 + b, 0)), kv_spec, kv_spec],
        out_specs=pl.BlockSpec((R, D), lambda b: (b, 0)),
        out_shape=jax.ShapeDtypeStruct((B * T, D), out_dtype),
        compiler_params=_cp("parallel"),
        name="xattn_decode",
    )(xq, mem_k, mem_v)


def _rope_tables(pos):
    half = DK_A // 2
    inv_freq = 1.0 / (ROPE_BASE ** (jnp.arange(half, dtype=f32) / half))
    ang = pos.astype(f32)[:, None] * inv_freq[None, :]
    cos, sin = jnp.cos(ang), jnp.sin(ang)
    return (jnp.tile(jnp.concatenate([cos, cos], axis=1), (1, H_A)),
            jnp.tile(jnp.concatenate([-sin, sin], axis=1), (1, H_A)))


def _w_in_segments():
    names = ("a_q", "a_k", "a_v", "a_g", "b_z", "b_xs", "b_bc", "b_dt", "c_q", "c_k", "c_v", "c_f",
             "d_q", "d_k", "d_v", "d_r", "d_lr")
    sizes = (256, 256, 512, 512, 512, BW, CONV_DIM - BW, H_B, 512, 512, 512, H_C, 256, 256, 512, 512, GLA_RANK)
    src, off = {}, 0
    for n, s in zip(names, sizes):
        src[n] = (off, s)
        off += s
    wide = [src[n] for n in ("a_v", "a_g", "b_z", "b_xs", "c_q", "c_k", "c_v", "d_v", "d_r",
                             "a_q", "a_k", "b_bc", "d_q", "d_k")]
    narrow = [src[n] for n in ("b_dt", "c_f", "d_lr")]
    return wide, narrow, off


def _w_in_prep_kernel(w_ref, mix_ref, gate_ref):
    wide, narrow, gate_start = _w_in_segments()
    dst = 0
    for start, width in wide:
        mix_ref[:, dst:dst + width] = w_ref[:, start:start + width].astype(bf16)
        dst += width
    rows = w_ref.shape[0]
    lane = lax.broadcasted_iota(jnp.int32, (rows, LANE), 1)
    small = jnp.zeros((rows, LANE), f32)
    for start, width in narrow:
        base = start - start % LANE
        in_seg = (lane >= start - base) & (lane < start - base + width)
        small = jnp.where(in_seg, w_ref[:, base:base + LANE], small)
    mix_ref[:, dst:dst + LANE] = small.astype(bf16)
    mix_ref[:, dst + LANE:] = jnp.zeros((rows, mix_ref.shape[1] - dst - LANE), bf16)
    gate_ref[...] = w_ref[:, gate_start:].astype(bf16)


def _permute_w_in(w_in):
    depth, D, n_in = w_in.shape
    n_gate = N_BRANCH * D
    _, narrow, gate_start = _w_in_segments()
    assert [s % LANE for s, _ in narrow] == [0, H_B, H_B + H_C] and gate_start + n_gate == n_in
    tk = _pick(D, (256, 128, 64, 32, 16))
    return pl.pallas_call(
        _w_in_prep_kernel,
        grid=(depth, D // tk),
        in_specs=[pl.BlockSpec((None, tk, n_in), lambda d, i: (d, i, 0))],
        out_specs=[pl.BlockSpec((None, tk, NP_COLS), lambda d, i: (d, i, 0)),
                   pl.BlockSpec((None, tk, n_gate), lambda d, i: (d, i, 0))],
        out_shape=[jax.ShapeDtypeStruct((depth, D, NP_COLS), bf16),
                   jax.ShapeDtypeStruct((depth, D, n_gate), bf16)],
        compiler_params=_cp("parallel", "parallel"),
        name="w_in_layout",
    )(w_in)


def _pad_lanes(x, start):
    depth, n = x.shape
    return jnp.zeros((depth, 1, LANE), f32).at[:, 0, start:start + n].set(x)


def kernel(x_prompt, x_sample, cache_fox_k, cache_fox_v, cache_fox_logf, cache_mem_k, cache_mem_v, state_ret, state_ssm, state_conv, state_gla, page_table, mem_prompt, w_in, b_gate, ret_norm_g, ssd_conv_w, ssd_conv_b, ssd_dt_bias, ssd_a_log, ssd_d, ssd_norm_g, fox_bf, gla_w_lr2, gla_b_lr, gla_norm_g, w_branch, w_out, norm_mix_g, norm_x_g, norm_mem_g, w_xq, w_mem_kv, w_xo, norm_ffn_g, w_ffn_up, w_ffn_down, norm_final_g):
    Bp, Lp, D = x_prompt.shape
    Bs, Ls, _ = x_sample.shape
    depth = w_in.shape[0]
    n_mem = mem_prompt.shape[1]
    past_len = page_table.shape[1] * PAGE_SIZE
    Tp, Ts = Bp * Lp, Bs * Ls

    w_mix, w_gate = _permute_w_in(w_in)
    w_branch_b = w_branch.astype(bf16)
    w_out_b = w_out.astype(bf16)
    w_xq_b = w_xq.astype(bf16)
    w_xo_b = w_xo.astype(bf16)
    w_kv_b = w_mem_kv.astype(bf16)
    w_up_b = w_ffn_up.astype(bf16)
    w_down_b = w_ffn_down.astype(bf16)
    b_gate4 = b_gate.reshape(depth, N_BRANCH, 1, D)
    g_mix = norm_mix_g.reshape(depth, 1, D)
    g_x = norm_x_g.reshape(depth, 1, D)
    g_mem = norm_mem_g.reshape(depth, 1, D)
    g_ffn = norm_ffn_g.reshape(depth, 1, D)
    ret_g = ret_norm_g.reshape(depth, 1, BW)
    gla_g = gla_norm_g.reshape(depth, 1, BW)
    ssd_g = ssd_norm_g.reshape(depth, 1, BW)
    conv_b = ssd_conv_b.reshape(depth, 1, CONV_DIM)
    dt_bias = _pad_lanes(ssd_dt_bias, 0)
    a_log = _pad_lanes(ssd_a_log, 0)
    d_skip = jnp.repeat(ssd_d, P_B, axis=1).reshape(depth, 1, BW)
    fox_b = _pad_lanes(fox_bf, H_B)
    w_lr2 = jnp.zeros((depth, LANE, H_D * DK_D), f32).at[:, H_B + H_C:H_B + H_C + GLA_RANK, :].set(
        gla_w_lr2).astype(bf16)
    b_lr = gla_b_lr.reshape(depth, 1, H_D * DK_D)
    cos_p, sin_p = _rope_tables(jnp.arange(Lp, dtype=jnp.int32))
    cos_s, sin_s = _rope_tables(past_len + jnp.arange(Ls, dtype=jnp.int32))
    ck_t = jnp.transpose(cache_fox_k, (0, 1, 3, 4, 2))
    cv_t = jnp.transpose(cache_fox_v, (0, 1, 3, 4, 2))
    clf_t = jnp.transpose(cache_fox_logf, (0, 1, 3, 2))
    z_ret = jnp.zeros((1, Bp, H_A, DK_A, DV_A), f32)
    z_ssm = jnp.zeros((1, Bp, H_B, N_B, P_B), f32)
    z_conv = jnp.zeros((1, Bp, CONV_W - 1, CONV_DIM), f32)
    z_gla = jnp.zeros((1, Bp, H_D, DK_D, DV_D), f32)
    mem2d = mem_prompt.reshape(Bp * n_mem, D)

    h = jnp.concatenate([x_prompt.reshape(Tp, D), x_sample.reshape(Ts, D)], axis=0)
    outs = {k: [] for k in ("ret_p", "ssm_p", "conv_p", "gla_p", "flf_p", "mk_p", "mv_p",
                            "ret_s", "ssm_s", "conv_s", "gla_s", "fk_s", "fv_s", "flf_s")}
    projs = []
    for l in range(depth):
        proj = _norm_matmul(h, g_mix, w_mix, l)

        pk = dict(B=Bp, L=Lp, row_off=0, out_dtype=bf16)
        sk = dict(B=Bs, L=Ls, row_off=Tp, out_dtype=f32)
        ya_p, ret_p = _retention(proj, cos_p, sin_p, z_ret, 0, ret_g, l, **pk)
        ya_s, ret_s = _retention(proj, cos_s, sin_s, state_ret, l, ret_g, l, **sk)
        yb_p, conv_p, ssm_p = _ssd(proj, z_conv, z_ssm, 0, ssd_conv_w, conv_b, dt_bias, a_log, d_skip,
                                   ssd_g, l, **pk)
        yb_s, conv_s, ssm_s = _ssd(proj, state_conv, state_ssm, l, ssd_conv_w, conv_b, dt_bias, a_log,
                                   d_skip, ssd_g, l, **sk)
        lf_p, c_p = _fox_prep(proj, fox_b, l, B=Bp, L=Lp, row_off=0)
        lf_s, c_s = _fox_prep(proj, fox_b, l, B=Bs, L=Ls, row_off=Tp)
        yc_p = _fox_prompt(proj, c_p, B=Bp, L=Lp, out_dtype=bf16)
        yc_s = _fox_sample(proj, c_s, ck_t, cv_t, clf_t, page_table, l, B=Bs, T=Ls, row_off=Tp, out_dtype=f32)
        yd_p, gla_p = _gla(proj, z_gla, 0, w_lr2, b_lr, gla_g, l, **pk)
        yd_s, gla_s = _gla(proj, state_gla, l, w_lr2, b_lr, gla_g, l, **sk)

        h = _merge(h, g_mix, (ya_p, yb_p, yc_p, yd_p), (ya_s, yb_s, yc_s, yd_s),
                   w_gate, b_gate4, w_branch_b, w_out_b, l)

        kv = _norm_matmul(mem2d, g_mem, w_kv_b, l, parts=2)
        mem_kv = kv.reshape(2, Bp, n_mem, D)
        xq = _norm_matmul(h, g_x, w_xq_b, l)
        o_p = _xattn(xq, mem_kv, mem_kv, 0, 1, B=Bp, L=Lp, row_off=0, out_dtype=bf16)
        o_s = _xattn_decode(xq, cache_mem_k, cache_mem_v, l, B=Bs, T=Ls, row_off=Tp, out_dtype=f32)
        h = _xo(h, o_p, o_s, w_xo_b, l)
        h = _ffn(h, g_ffn, w_up_b, w_down_b, l)

        projs.append(proj)
        ckv = proj[Tp:, C_CK * BW:(C_CV + 1) * BW]
        outs["ret_p"].append(ret_p); outs["ssm_p"].append(ssm_p); outs["conv_p"].append(conv_p)
        outs["gla_p"].append(gla_p)
        outs["flf_p"].append(lf_p.reshape(Bp, Lp, H_C))
        outs["mk_p"].append(kv[0].reshape(Bp, n_mem, H_X, D // H_X))
        outs["mv_p"].append(kv[1].reshape(Bp, n_mem, H_X, D // H_X))
        outs["ret_s"].append(ret_s); outs["ssm_s"].append(ssm_s); outs["conv_s"].append(conv_s)
        outs["gla_s"].append(gla_s)
        outs["fk_s"].append(ckv[:, :BW].reshape(Bs, Ls, H_C, DH_C))
        outs["fv_s"].append(ckv[:, BW:].reshape(Bs, Ls, H_C, DH_C))
        outs["flf_s"].append(lf_s.reshape(Bs, Ls, H_C))

    y = _final_norm(h, norm_final_g.reshape(1, D))
    st = {k: jnp.stack(v, axis=0) for k, v in outs.items()}
    fk_t, fv_t = _fox_kv_out(projs, B=Bp, L=Lp)
    st["fk_p"] = jnp.transpose(fk_t, (0, 1, 4, 2, 3))
    st["fv_p"] = jnp.transpose(fv_t, (0, 1, 4, 2, 3))
    return (y[:Tp].reshape(Bp, Lp, D), y[Tp:].reshape(Bs, Ls, D),
            st["ret_p"], st["ssm_p"], st["conv_p"], st["gla_p"], st["fk_p"], st["fv_p"], st["flf_p"],
            st["mk_p"], st["mv_p"],
            st["ret_s"], st["ssm_s"], st["conv_s"], st["gla_s"], st["fk_s"], st["fv_s"], st["flf_s"])
```
